```python
import math
import jax, jax.numpy as jnp
from jax import lax
import numpy as np

D_MODEL = 4096
BATCH = 1
SEQ = 8192
DEPTH = 2

GRID_W = 64
CTX_LEN = 256
N_BRANCH = 3
BRANCH_W = 2048
EPS = 1e-6

SSM_HEADS = 32
SSM_HEAD_DIM = 64
SSM_GROUPS = 4
SSM_STATE = 128
SSM_CHUNK = 128
CONV_K = 5
SSM_GN = SSM_GROUPS * SSM_STATE
XBC_W = BRANCH_W + 2 * SSM_GN

ATT_HEADS = 16
KV_HEADS = 4
ATT_GROUP = ATT_HEADS // KV_HEADS
HEAD_DIM = 128
Q_BLOCK = 128
ROPE_THETA = 10000.0

NA_HEADS = 16
NA_WIN_R = 8
NA_WIN_C = 16

IN_SIZES = (
    XBC_W,
    BRANCH_W,
    2 * SSM_HEADS,
    BRANCH_W,
    KV_HEADS * HEAD_DIM,
    KV_HEADS * HEAD_DIM,
    BRANCH_W,
    BRANCH_W,
    BRANCH_W,
    BRANCH_W,
    BRANCH_W,
    N_BRANCH * D_MODEL,
)
IN_COLS = sum(IN_SIZES)

kernel_name = "hybrid_ssd_gqa_natten_dit_block"


def rmsnorm(x, w):
    xf = x.astype(jnp.float32)
    y = xf * lax.rsqrt(jnp.mean(xf * xf, axis=-1, keepdims=True) + EPS)
    return (y * w.astype(jnp.float32)).astype(x.dtype)


def split_cols(p):
    outs, start = [], 0
    for s in IN_SIZES:
        outs.append(p[..., start:start + s])
        start += s
    return outs


def dwconv_centred(u, w, b):
    pad = CONV_K // 2
    out = lax.conv_general_dilated(u, w[:, None, :], window_strides=(1,), padding=[(pad, pad)],
                                   dimension_numbers=('NWC', 'WIO', 'NWC'),
                                   feature_group_count=u.shape[-1])
    return out + b


def segsum_exp(a):
    l = a.shape[-1]
    diff = a[..., :, None] - a[..., None, :]
    mask = jnp.tril(jnp.ones((l, l), dtype=bool))
    return jnp.exp(jnp.where(mask, diff, -jnp.inf))


def ssd_scan(xh, dt, a, bm, cm, h0):
    b, t, nh, p = xh.shape
    g, n = bm.shape[-2:]
    r = nh // g
    l = SSM_CHUNK
    nc = t // l
    x = xh.reshape(b, nc, l, g, r, p)
    dtc = dt.reshape(b, nc, l, g, r)
    bc = bm.reshape(b, nc, l, g, n)
    cc = cm.reshape(b, nc, l, g, n)
    acs = jnp.cumsum(dtc * a.reshape(g, r), axis=2)
    decay = segsum_exp(jnp.moveaxis(acs, 2, -1))
    cb = jnp.einsum('bclgn,bcsgn->bcgls', cc, bc)
    m = cb[:, :, :, None] * decay * jnp.moveaxis(dtc, 2, -1)[..., None, :]
    y_diag = jnp.einsum('bcgrls,bcsgrp->bclgrp', m, x)
    a_last = acs[:, :, -1]
    w_state = jnp.exp(a_last[:, :, None] - acs) * dtc
    states = jnp.einsum('bclgn,bclgr,bclgrp->bcgrpn', bc, w_state, x)

    def step(h, inp):
        s_c, a_c = inp
        return jnp.exp(a_c)[..., None, None] * h + s_c, h

    h_final, h_start = lax.scan(step, h0.reshape(b, g, r, p, n),
                                (jnp.moveaxis(states, 1, 0), jnp.moveaxis(a_last, 1, 0)))
    h_start = jnp.moveaxis(h_start, 0, 1)
    y_off = jnp.einsum('bclgn,bcgrpn,bclgr->bclgrp', cc, h_start, jnp.exp(acs))
    y = (y_diag + y_off).reshape(b, t, nh, p)
    return y, h_final.reshape(b, nh, p, n)


def ssd_inputs(xbc, dt_raw, conv_w, conv_b, dt_bias):
    b, t, _ = xbc.shape
    u = jax.nn.silu(dwconv_centred(xbc, conv_w, conv_b)).astype(jnp.float32)
    xs = u[..., :BRANCH_W].reshape(b, t, SSM_HEADS, SSM_HEAD_DIM)
    bm = u[..., BRANCH_W:BRANCH_W + SSM_GN].reshape(b, t, SSM_GROUPS, SSM_STATE)
    cm = u[..., BRANCH_W + SSM_GN:].reshape(b, t, SSM_GROUPS, SSM_STATE)
    dt = jax.nn.softplus(dt_raw.astype(jnp.float32).reshape(b, t, 2, SSM_HEADS)
                         + dt_bias.astype(jnp.float32))
    return xs, bm, cm, dt


def ssd_bidirectional(lat, ctx, a_log, d_skip):
    a = -jnp.exp(a_log.astype(jnp.float32))
    d = d_skip.astype(jnp.float32)[:, None]
    xl, bl, cl, dtl = lat
    xc, bc, cc, dtc = ctx
    b, tl = xl.shape[:2]
    tc = xc.shape[1]
    h0 = jnp.zeros((b, SSM_HEADS, SSM_HEAD_DIM, SSM_STATE), jnp.float32)
    rev = lambda u: jnp.flip(u, axis=1)
    yc_f, hc_f = ssd_scan(xc, dtc[:, :, 0], a[0], bc, cc, h0)
    yl_f, _ = ssd_scan(xl, dtl[:, :, 0], a[0], bl, cl, hc_f)
    yc_b, hc_b = ssd_scan(rev(xc), rev(dtc[:, :, 1]), a[1], rev(bc), rev(cc), h0)
    yl_b, _ = ssd_scan(rev(xl), rev(dtl[:, :, 1]), a[1], rev(bl), rev(cl), hc_b)
    yl = yl_f + rev(yl_b) + d * xl
    yc = yc_f + rev(yc_b) + d * xc
    return yl.reshape(b, tl, BRANCH_W), yc.reshape(b, tc, BRANCH_W)


def axial_rope(t):
    pos = jnp.arange(t, dtype=jnp.int32)
    row = (pos // GRID_W).astype(jnp.float32)
    col = (pos % GRID_W).astype(jnp.float32)
    n_freq = HEAD_DIM // 4
    inv_freq = ROPE_THETA ** (-jnp.arange(n_freq, dtype=jnp.float32) / n_freq)
    ang = jnp.concatenate([row[:, None] * inv_freq, col[:, None] * inv_freq], axis=-1)
    return jnp.cos(ang), jnp.sin(ang)


def apply_rope(x, cos, sin):
    t, d = x.shape[1], x.shape[-1]
    shp = (t,) + (1,) * (x.ndim - 3) + (d // 2,)
    cos, sin = cos.reshape(shp), sin.reshape(shp)
    xp = x.astype(jnp.float32).reshape(*x.shape[:-1], d // 2, 2)
    x1, x2 = xp[..., 0], xp[..., 1]
    out = jnp.stack([x1 * cos - x2 * sin, x1 * sin + x2 * cos], axis=-1)
    return out.reshape(x.shape).astype(x.dtype)


def attend(q, k, v):
    s = jnp.einsum('bqgrd,bsgd->bgrqs', q, k).astype(jnp.float32) * (q.shape[-1] ** -0.5)
    p = jax.nn.softmax(s, axis=-1).astype(v.dtype)
    return jnp.einsum('bgrqs,bsgd->bqgrd', p, v)


def blocked_attend(q, k, v):
    b, t = q.shape[:2]
    nb = t // Q_BLOCK
    qb = jnp.moveaxis(q.reshape(b, nb, Q_BLOCK, *q.shape[2:]), 1, 0)
    ob = lax.map(lambda blk: attend(blk, k, v), qb)
    return jnp.moveaxis(ob, 0, 1).reshape(q.shape)


def gqa_heads(q, k, v, q_norm_w, k_norm_w):
    b, t = q.shape[:2]
    q = rmsnorm(q.reshape(b, t, KV_HEADS, ATT_GROUP, HEAD_DIM), q_norm_w)
    k = rmsnorm(k.reshape(b, t, KV_HEADS, HEAD_DIM), k_norm_w)
    v = v.reshape(b, t, KV_HEADS, HEAD_DIM)
    return q, k, v


def neighbourhood_attention(q, k, v, k_ctx, v_ctx, rpb):
    b, t, h, d = q.shape
    rows = t // GRID_W
    wr = min(NA_WIN_R, rows)
    wc = NA_WIN_C
    n_loc = wr * wc
    scale = d ** -0.5
    qg = q.reshape(b, rows, GRID_W, h, d)
    kg = k.reshape(b, rows, GRID_W, h, d)
    vg = v.reshape(b, rows, GRID_W, h, d)
    col = jnp.arange(GRID_W)
    col_idx = jnp.clip(col - wc // 2, 0, GRID_W - wc)[:, None] + jnp.arange(wc)
    col_bias_idx = col_idx - col[:, None] + (NA_WIN_C - 1)

    def row_block(r):
        r0 = jnp.clip(r - wr // 2, 0, rows - wr)
        kr = lax.dynamic_slice_in_dim(kg, r0, wr, axis=1)[:, :, col_idx]
        vr = lax.dynamic_slice_in_dim(vg, r0, wr, axis=1)[:, :, col_idx]
        qr = lax.dynamic_index_in_dim(qg, r, axis=1, keepdims=False)
        row_bias_idx = r0 + jnp.arange(wr) - r + (NA_WIN_R - 1)
        bias = jnp.transpose(rpb[:, row_bias_idx][:, :, col_bias_idx], (0, 2, 1, 3))
        s_loc = jnp.einsum('bwhd,bxwkhd->bhwxk', qr, kr).astype(jnp.float32) * scale + bias.astype(jnp.float32)
        s_ctx = jnp.einsum('bwhd,bshd->bhws', qr, k_ctx).astype(jnp.float32) * scale
        s = jnp.concatenate([s_loc.reshape(b, h, GRID_W, n_loc), s_ctx], axis=-1)
        p = jax.nn.softmax(s, axis=-1).astype(v.dtype)
        p_loc = p[..., :n_loc].reshape(b, h, GRID_W, wr, wc)
        return (jnp.einsum('bhwxk,bxwkhd->bwhd', p_loc, vr)
                + jnp.einsum('bhws,bshd->bwhd', p[..., n_loc:], v_ctx))

    out = lax.map(row_block, jnp.arange(rows))
    return jnp.moveaxis(out, 0, 1).reshape(b, t, h * d)


def merge_branches(branches, g_raw, w_up, w_out):
    b, t, _ = g_raw.shape
    gates = jax.nn.sigmoid(g_raw.reshape(b, t, N_BRANCH, D_MODEL))
    up = jnp.einsum('btiw,iwd->btid', jnp.stack(branches, axis=2), w_up)
    return jnp.sum(gates * up, axis=2) @ w_out


def hybrid_layer(x, ctx, c, c_ctx, norm_w, w_mod, b_mod, w_in, b_in, conv_w, conv_b,
                 a_log, dt_bias, d_skip, ssm_norm_w, q_norm_w, k_norm_w, rpb, w_up, w_out,
                 update_ctx):
    b, t, _ = x.shape
    lc = ctx.shape[1]
    mod_x = jax.nn.silu(c) @ w_mod + b_mod
    mod_c = jax.nn.silu(c_ctx) @ w_mod + b_mod
    shift_x, scale_x, gate_x = jnp.split(mod_x[:, None, :], 3, axis=-1)
    shift_c, scale_c, gate_c = jnp.split(mod_c, 3, axis=-1)
    hx = rmsnorm(x, norm_w) * (1 + scale_x) + shift_x
    hc = rmsnorm(ctx, norm_w) * (1 + scale_c) + shift_c
    (xbc_x, z_x, dt_x, qa_x, ka_x, va_x, ga_x, qn_x, kn_x, vn_x, gn_x, gm_x) = split_cols(hx @ w_in + b_in)
    (xbc_c, z_c, dt_c, qa_c, ka_c, va_c, ga_c, qn_c, kn_c, vn_c, gn_c, gm_c) = split_cols(hc @ w_in + b_in)

    ssd_x = ssd_inputs(xbc_x, dt_x, conv_w, conv_b, dt_bias)
    ssd_c = ssd_inputs(xbc_c, dt_c, conv_w, conv_b, dt_bias)
    y_ssd_x, y_ssd_c = ssd_bidirectional(ssd_x, ssd_c, a_log, d_skip)
    br_a_x = rmsnorm(y_ssd_x * jax.nn.silu(z_x.astype(jnp.float32)), ssm_norm_w).astype(x.dtype)

    cos, sin = axial_rope(t)
    qx, kx, vx = gqa_heads(qa_x, ka_x, va_x, q_norm_w, k_norm_w)
    qc, kc, vc = gqa_heads(qa_c, ka_c, va_c, q_norm_w, k_norm_w)
    qx = apply_rope(qx, cos, sin)
    kx = apply_rope(kx, cos, sin)
    k_all = jnp.concatenate([kc, kx], axis=1)
    v_all = jnp.concatenate([vc, vx], axis=1)
    o_att_x = blocked_attend(qx, k_all, v_all).reshape(b, t, BRANCH_W).astype(x.dtype)
    br_b_x = jax.nn.silu(ga_x) * o_att_x

    kn_ctx = kn_c.reshape(b, lc, NA_HEADS, HEAD_DIM)
    vn_ctx = vn_c.reshape(b, lc, NA_HEADS, HEAD_DIM)
    o_na_x = neighbourhood_attention(qn_x.reshape(b, t, NA_HEADS, HEAD_DIM),
                                     kn_x.reshape(b, t, NA_HEADS, HEAD_DIM),
                                     vn_x.reshape(b, t, NA_HEADS, HEAD_DIM),
                                     kn_ctx, vn_ctx, rpb).astype(x.dtype)
    br_c_x = jax.nn.silu(gn_x) * o_na_x

    x_new = x + gate_x * merge_branches((br_a_x, br_b_x, br_c_x), gm_x, w_up, w_out)

    if update_ctx:
        br_a_c = rmsnorm(y_ssd_c * jax.nn.silu(z_c.astype(jnp.float32)), ssm_norm_w).astype(ctx.dtype)
        o_att_c = attend(qc, kc, vc).reshape(b, lc, BRANCH_W).astype(ctx.dtype)
        br_b_c = jax.nn.silu(ga_c) * o_att_c
        o_na_c = attend(qn_c.reshape(b, lc, NA_HEADS, 1, HEAD_DIM), kn_ctx, vn_ctx)
        br_c_c = jax.nn.silu(gn_c) * o_na_c.reshape(b, lc, BRANCH_W).astype(ctx.dtype)
        ctx_new = ctx + gate_c * merge_branches((br_a_c, br_b_c, br_c_c), gm_c, w_up, w_out)
    else:
        ctx_new = ctx
    return x_new, ctx_new


def setup_inputs(seed: int = 0) -> dict:
    key = jax.random.key(seed)
    ks = jax.random.split(key, 21)
    f32 = jnp.float32

    def nrm(k, shape, scale):
        return jax.random.normal(k, shape, f32) * scale

    dt0 = jnp.exp(jax.random.uniform(ks[12], (DEPTH, 2, SSM_HEADS), f32, math.log(1e-3), math.log(1e-1)))
    return {
        "x": nrm(ks[0], (BATCH, SEQ, D_MODEL), 1.0),
        "c": nrm(ks[1], (BATCH, D_MODEL), 1.0),
        "ctx": nrm(ks[2], (BATCH, CTX_LEN, D_MODEL), 1.0),
        "c_ctx": nrm(ks[3], (D_MODEL,), 1.0),
        "norm_w": 1.0 + nrm(ks[4], (DEPTH, D_MODEL), 0.02),
        "w_mod": nrm(ks[5], (DEPTH, D_MODEL, 3 * D_MODEL), 0.5 * D_MODEL ** -0.5),
        "b_mod": nrm(ks[6], (DEPTH, 3 * D_MODEL), 0.02),
        "w_in": nrm(ks[7], (DEPTH, D_MODEL, IN_COLS), D_MODEL ** -0.5),
        "b_in": nrm(ks[8], (DEPTH, IN_COLS), 0.02),
        "conv_w": nrm(ks[9], (DEPTH, CONV_K, XBC_W), CONV_K ** -0.5),
        "conv_b": nrm(ks[10], (DEPTH, XBC_W), 0.02),
        "a_log": jnp.log(jax.random.uniform(ks[11], (DEPTH, 2, SSM_HEADS), f32, 1.0, 16.0)),
        "dt_bias": dt0 + jnp.log(-jnp.expm1(-dt0)),
        "d_skip": 1.0 + nrm(ks[13], (DEPTH, SSM_HEADS), 0.1),
        "ssm_norm_w": 1.0 + nrm(ks[14], (DEPTH, BRANCH_W), 0.02),
        "q_norm_w": 1.0 + nrm(ks[15], (DEPTH, HEAD_DIM), 0.02),
        "k_norm_w": 1.0 + nrm(ks[16], (DEPTH, HEAD_DIM), 0.02),
        "rpb": nrm(ks[17], (DEPTH, NA_HEADS, 2 * NA_WIN_R - 1, 2 * NA_WIN_C - 1), 0.1),
        "w_up": nrm(ks[18], (DEPTH, N_BRANCH, BRANCH_W, D_MODEL), BRANCH_W ** -0.5),
        "w_out": nrm(ks[19], (DEPTH, D_MODEL, D_MODEL), D_MODEL ** -0.5),
        "final_norm_w": 1.0 + nrm(ks[20], (D_MODEL,), 0.02),
    }


def reference(x, c, ctx, c_ctx, norm_w, w_mod, b_mod, w_in, b_in, conv_w, conv_b, a_log, dt_bias,
              d_skip, ssm_norm_w, q_norm_w, k_norm_w, rpb, w_up, w_out, final_norm_w):
    for layer in range(DEPTH):
        x, ctx = hybrid_layer(x, ctx, c, c_ctx, norm_w[layer], w_mod[layer], b_mod[layer], w_in[layer],
                              b_in[layer], conv_w[layer], conv_b[layer], a_log[layer], dt_bias[layer],
                              d_skip[layer], ssm_norm_w[layer], q_norm_w[layer], k_norm_w[layer],
                              rpb[layer], w_up[layer], w_out[layer], update_ctx=layer < DEPTH - 1)
    return rmsnorm(x, final_norm_w)
```

```python
import functools
import math

import numpy as np
import jax
import jax.numpy as jnp
from jax import lax
from jax.experimental import pallas as pl
from jax.experimental.pallas import tpu as pltpu

F32 = jnp.float32
BF16 = jnp.bfloat16

D_MODEL = 4096
DEPTH = 2
GRID_W = 64
N_BRANCH = 3
BRANCH_W = 2048
EPS = 1e-6

SSM_HEADS = 32
SSM_HEAD_DIM = 64
SSM_GROUPS = 4
SSM_STATE = 128
SSM_CHUNK = 128
CONV_K = 5
SSM_GN = SSM_GROUPS * SSM_STATE
XBC_W = BRANCH_W + 2 * SSM_GN
GROUP_W = (SSM_HEADS // SSM_GROUPS) * SSM_HEAD_DIM

ATT_HEADS = 16
KV_HEADS = 4
ATT_GROUP = ATT_HEADS // KV_HEADS
HEAD_DIM = 128
ROPE_THETA = 10000.0

NA_HEADS = 16
NA_WIN_R = 8
NA_WIN_C = 16

LANES = 128
MIB = 1024 * 1024

_SRC_DT = XBC_W + BRANCH_W
_SRC_QA = _SRC_DT + 2 * SSM_HEADS
_SRC_VA = _SRC_QA + BRANCH_W + KV_HEADS * HEAD_DIM
OFF_XBC = 0
OFF_Z = OFF_XBC + XBC_W
OFF_QA = OFF_Z + BRANCH_W
OFF_KA = OFF_QA + BRANCH_W
OFF_VA = OFF_KA + KV_HEADS * HEAD_DIM
OFF_GA = OFF_VA + KV_HEADS * HEAD_DIM
OFF_QN = OFF_GA + BRANCH_W
OFF_KN = OFF_QN + BRANCH_W
OFF_VN = OFF_KN + BRANCH_W
OFF_GN = OFF_VN + BRANCH_W
OFF_GM = OFF_GN + BRANCH_W
P_COLS = OFF_GM + N_BRANCH * D_MODEL
QK_W = BRANCH_W + KV_HEADS * HEAD_DIM

NEG_BIAS = -1e30


def _params(sem, vmem_mib):
    return pltpu.CompilerParams(dimension_semantics=sem, vmem_limit_bytes=vmem_mib * MIB)


def _pick(n, candidates):
    for c in candidates:
        if n % c == 0:
            return c
    raise ValueError(f"no tile for {n} in {candidates}")


def _sigmoid(v):
    return 1.0 / (1.0 + jnp.exp(-v))


def _silu(v):
    return v * _sigmoid(v)


def _mod_kernel(s_ref, w_ref, b_ref, o_ref):
    tn = w_ref.shape[1]
    k = w_ref.shape[0]
    o_ref[...] = jnp.zeros(o_ref.shape, F32)
    for v in range(2):
        s = _silu(s_ref[v])
        for jb in range(tn // LANES):
            sl = slice(jb * LANES, (jb + 1) * LANES)
            prod = w_ref[:, sl] * s
            part = prod.reshape(k // 8, 8, LANES).sum(axis=0)
            o_ref[v:v + 1, sl] = part.sum(axis=0, keepdims=True) + b_ref[:, sl]


def modulation(sb, w_mod, b_mod):
    k, n = w_mod.shape
    tn = 512
    return pl.pallas_call(
        _mod_kernel,
        grid=(n // tn,),
        in_specs=[pl.BlockSpec((2, k, LANES), lambda j: (0, 0, 0)),
                  pl.BlockSpec((k, tn), lambda j: (0, j)),
                  pl.BlockSpec((1, tn), lambda j: (0, j))],
        out_specs=pl.BlockSpec((8, tn), lambda j: (0, j)),
        out_shape=jax.ShapeDtypeStruct((8, n), F32),
        compiler_params=_params(("arbitrary",), 40),
        name="modulation",
    )(sb, w_mod, b_mod.reshape(1, n))


def _norm_kernel(x_ref, c_ref, nw_ref, mod_ref, o_ref, *, n_lat_blocks):
    i = pl.program_id(0)

    def emit(src_ref, row):
        xv = src_ref[...]
        y = xv * lax.rsqrt(jnp.mean(xv * xv, axis=-1, keepdims=True) + EPS) * nw_ref[...]
        shift = mod_ref[row:row + 1, 0:D_MODEL]
        scale = mod_ref[row:row + 1, D_MODEL:2 * D_MODEL]
        o_ref[...] = (y * (1.0 + scale) + shift).astype(o_ref.dtype)

    @pl.when(i < n_lat_blocks)
    def _():
        emit(x_ref, 0)

    @pl.when(i >= n_lat_blocks)
    def _():
        emit(c_ref, 1)


def norm_modulate(x, ctx, norm_w, mod):
    t, d = x.shape
    lc = ctx.shape[0]
    tr = 256
    n_lat = t // tr
    n_ctx = lc // tr
    return pl.pallas_call(
        functools.partial(_norm_kernel, n_lat_blocks=n_lat),
        grid=(n_lat + n_ctx,),
        in_specs=[pl.BlockSpec((tr, d), lambda i: (jnp.minimum(i, n_lat - 1), 0)),
                  pl.BlockSpec((tr, d), lambda i: (jnp.maximum(i - n_lat, 0), 0)),
                  pl.BlockSpec((1, d), lambda i: (0, 0)),
                  pl.BlockSpec((8, 3 * d), lambda i: (0, 0))],
        out_specs=pl.BlockSpec((tr, d), lambda i: (i, 0)),
        out_shape=jax.ShapeDtypeStruct((t + lc, d), BF16),
        compiler_params=_params(("arbitrary",), 40),
        name="norm_modulate",
    )(x, ctx, norm_w.reshape(1, d), mod)


def _matmul_bias_kernel(h_ref, w_ref, b_ref, o_ref):
    acc = jnp.dot(h_ref[...], w_ref[...], preferred_element_type=F32)
    o_ref[...] = (acc + b_ref[...]).astype(o_ref.dtype)


def matmul_bias(h, w, b, out_dtype, name):
    m, k = h.shape
    n = w.shape[1]
    tm = _pick(m, (1408, 1280, 1024, 768, 640, 512, 256, 128))
    tn = _pick(n, (512, 256, 128))
    return pl.pallas_call(
        _matmul_bias_kernel,
        grid=(m // tm, n // tn),
        in_specs=[pl.BlockSpec((tm, k), lambda i, j: (i, 0)),
                  pl.BlockSpec((k, tn), lambda i, j: (0, j)),
                  pl.BlockSpec((1, tn), lambda i, j: (0, j))],
        out_specs=pl.BlockSpec((tm, tn), lambda i, j: (i, j)),
        out_shape=jax.ShapeDtypeStruct((m, n), out_dtype),
        compiler_params=_params(("arbitrary", "arbitrary"), 48),
        name=name,
    )(h, w, b.reshape(1, n))


def _conv_kernel(prev_ref, cur_ref, next_ref, w_ref, b_ref, o_ref, *, n_lat_chunks, n_chunks):
    c = pl.program_id(0)
    has_prev = jnp.logical_and(c != 0, c != n_lat_chunks)
    has_next = jnp.logical_and(c != n_lat_chunks - 1, c != n_chunks - 1)
    pad = CONV_K // 2
    cw = 512
    for jb in range(XBC_W // cw):
        sl = slice(jb * cw, (jb + 1) * cw)
        cur = cur_ref[:, sl].astype(F32)
        pv = prev_ref[SSM_CHUNK - 16:SSM_CHUNK, sl].astype(F32)[8:16]
        nx = next_ref[0:16, sl].astype(F32)[0:8]
        pv = jnp.where(has_prev, pv, 0.0)
        nx = jnp.where(has_next, nx, 0.0)
        ext = jnp.concatenate([pv, cur, nx], axis=0)
        acc = jnp.broadcast_to(b_ref[:, sl], (SSM_CHUNK, cw))
        for kk in range(CONV_K):
            lo = 8 + kk - pad
            acc = acc + w_ref[kk:kk + 1, sl] * ext[lo:lo + SSM_CHUNK]
        o_ref[:, sl] = _silu(acc).astype(o_ref.dtype)


def ssd_conv(p, conv_w, conv_b, n_lat_chunks):
    r = p.shape[0]
    n_chunks = r // SSM_CHUNK

    def prev_map(c):
        bad = jnp.logical_or(c == 0, c == n_lat_chunks)
        return (jnp.where(bad, c, c - 1), 0)

    def next_map(c):
        bad = jnp.logical_or(c == n_lat_chunks - 1, c == n_chunks - 1)
        return (jnp.where(bad, c, c + 1), 0)

    blk = (SSM_CHUNK, XBC_W)
    return pl.pallas_call(
        functools.partial(_conv_kernel, n_lat_chunks=n_lat_chunks, n_chunks=n_chunks),
        grid=(n_chunks,),
        in_specs=[pl.BlockSpec(blk, prev_map),
                  pl.BlockSpec(blk, lambda c: (c, 0)),
                  pl.BlockSpec(blk, next_map),
                  pl.BlockSpec((8, XBC_W), lambda c: (0, 0)),
                  pl.BlockSpec((1, XBC_W), lambda c: (0, 0))],
        out_specs=pl.BlockSpec(blk, lambda c: (c, 0)),
        out_shape=jax.ShapeDtypeStruct((r, XBC_W), BF16),
        compiler_params=_params(("arbitrary",), 32),
        name="ssd_conv",
    )(p, p, p, jnp.pad(conv_w, ((0, 8 - CONV_K), (0, 0))), conv_b.reshape(1, XBC_W))


def _dot3(tri, v):
    hi = v.astype(BF16)
    r1 = v - hi.astype(F32)
    mid = r1.astype(BF16)
    lo = (r1 - mid.astype(F32)).astype(BF16)
    return (jnp.dot(tri, hi, preferred_element_type=F32)
            + jnp.dot(tri, mid, preferred_element_type=F32)
            + jnp.dot(tri, lo, preferred_element_type=F32))


def _pair_cols(v, ha, lane_lo):
    rows = v.shape[0]
    a = jnp.broadcast_to(v[:, ha:ha + 1], (rows, LANES))
    b = jnp.broadcast_to(v[:, ha + 1:ha + 2], (rows, LANES))
    return jnp.where(lane_lo, a, b)


def _ssd_chunk(u_ref, dt_ref, cst_ref, ht_ref, emit, *, reverse):
    L = SSM_CHUNK
    hoff = SSM_HEADS if reverse else 0
    li = lax.broadcasted_iota(jnp.int32, (L, L), 0)
    si = lax.broadcasted_iota(jnp.int32, (L, L), 1)
    keep = (li <= si) if reverse else (li >= si)
    tri = jnp.where((si >= li) if reverse else (si <= li), 1.0, 0.0).astype(BF16)
    lane_lo = lax.broadcasted_iota(jnp.int32, (L, LANES), 1) < SSM_HEAD_DIM
    lane_lo1 = lax.broadcasted_iota(jnp.int32, (1, LANES), 1) < SSM_HEAD_DIM

    raw = dt_ref[...] + cst_ref[0:1, :]
    dtv = jnp.maximum(raw, 0.0) + jnp.log1p(jnp.exp(-jnp.abs(raw)))
    da = dtv * cst_ref[1:2, :]
    acs = _dot3(tri, da)
    acs_t = acs.T
    dt_t = dtv.T
    a_last = acs[0:1, :] if reverse else acs[L - 1:L, :]
    wst = jnp.exp(a_last - acs) * dtv
    eacs = jnp.exp(acs)
    e_last = jnp.exp(a_last)

    pairs = GROUP_W // LANES
    for g in range(SSM_GROUPS):
        bg = u_ref[:, BRANCH_W + g * SSM_STATE:BRANCH_W + (g + 1) * SSM_STATE]
        cg = u_ref[:, BRANCH_W + SSM_GN + g * SSM_STATE:BRANCH_W + SSM_GN + (g + 1) * SSM_STATE]
        cb = lax.dot_general(cg, bg, (((1,), (1,)), ((), ())), preferred_element_type=F32)
        bg_t = bg.astype(F32).T.astype(BF16)
        y_parts, xw_parts, e_parts, d_parts = [], [], [], []
        for pi in range(pairs):
            ha = hoff + g * (2 * pairs) + 2 * pi
            xp = u_ref[:, g * GROUP_W + pi * LANES:g * GROUP_W + (pi + 1) * LANES]
            zero = jnp.zeros_like(xp)
            y_pair = None
            for hh, xm in ((ha, jnp.where(lane_lo, xp, zero)), (ha + 1, jnp.where(lane_lo, zero, xp))):
                diff = acs[:, hh:hh + 1] - acs_t[hh:hh + 1, :]
                dec = jnp.exp(jnp.where(keep, diff, -jnp.inf))
                m = (cb * dec * dt_t[hh:hh + 1, :]).astype(BF16)
                yh = jnp.dot(m, xm, preferred_element_type=F32)
                y_pair = yh if y_pair is None else y_pair + yh
            y_parts.append(y_pair)
            xw_parts.append((xp.astype(F32) * _pair_cols(wst, ha, lane_lo)).astype(BF16))
            e_parts.append(_pair_cols(eacs, ha, lane_lo))
            d_parts.append(_pair_cols(e_last, ha, lane_lo1))
        hs = ht_ref[g]
        y_off = jnp.dot(cg, hs.astype(BF16), preferred_element_type=F32) * jnp.concatenate(e_parts, axis=1)
        s_new = jnp.dot(bg_t, jnp.concatenate(xw_parts, axis=1), preferred_element_type=F32)
        ht_ref[g] = hs * jnp.concatenate(d_parts, axis=1) + s_new
        emit(g, jnp.concatenate(y_parts, axis=1) + y_off)


def _ssd_fwd_kernel(u_ref, dt_ref, cst_ref, o_ref, ht_ref):
    @pl.when(pl.program_id(0) == 0)
    def _():
        ht_ref[...] = jnp.zeros(ht_ref.shape, F32)

    def emit(g, y):
        o_ref[:, g * GROUP_W:(g + 1) * GROUP_W] = y

    _ssd_chunk(u_ref, dt_ref, cst_ref, ht_ref, emit, reverse=False)


def _ssd_bwd_kernel(u_ref, dt_ref, cst_ref, yf_ref, z0_ref, z1_ref, dsk_ref, nw_ref, o_ref, ht_ref, y_ref):
    @pl.when(pl.program_id(0) == 0)
    def _():
        ht_ref[...] = jnp.zeros(ht_ref.shape, F32)

    half = SSM_GROUPS // 2

    def emit(g, y):
        sl = slice(g * GROUP_W, (g + 1) * GROUP_W)
        z_ref = z0_ref if g < half else z1_ref
        zsl = slice((g % half) * GROUP_W, (g % half + 1) * GROUP_W)
        tot = y + yf_ref[:, sl] + dsk_ref[:, sl] * u_ref[:, sl].astype(F32)
        y_ref[:, sl] = tot * _silu(z_ref[:, zsl].astype(F32))

    _ssd_chunk(u_ref, dt_ref, cst_ref, ht_ref, emit, reverse=True)
    yv = y_ref[...]
    o_ref[...] = (yv * lax.rsqrt(jnp.mean(yv * yv, axis=-1, keepdims=True) + EPS) * nw_ref[...]).astype(o_ref.dtype)


def ssd_branch(u, dt_raw, cst, p, d_exp, ssm_norm_w, n_ctx_chunks):
    r = u.shape[0]
    n_chunks = r // SSM_CHUNK
    n_lat = n_chunks - n_ctx_chunks
    state = pltpu.VMEM((SSM_GROUPS, SSM_STATE, GROUP_W), F32)

    def fwd_idx(i):
        return jnp.where(i < n_ctx_chunks, n_lat + i, i - n_ctx_chunks)

    def bwd_idx(i):
        return n_chunks - 1 - i

    ublk = (SSM_CHUNK, XBC_W)
    yf = pl.pallas_call(
        _ssd_fwd_kernel,
        grid=(n_chunks,),
        in_specs=[pl.BlockSpec(ublk, lambda i: (fwd_idx(i), 0)),
                  pl.BlockSpec((SSM_CHUNK, LANES), lambda i: (fwd_idx(i), 0)),
                  pl.BlockSpec((8, LANES), lambda i: (0, 0))],
        out_specs=pl.BlockSpec((SSM_CHUNK, BRANCH_W), lambda i: (fwd_idx(i), 0)),
        out_shape=jax.ShapeDtypeStruct((r, BRANCH_W), F32),
        scratch_shapes=[state],
        compiler_params=_params(("arbitrary",), 32),
        name="ssd_fwd",
    )(u, dt_raw, cst)

    zw = BRANCH_W // 2
    z0 = OFF_Z // zw
    return pl.pallas_call(
        _ssd_bwd_kernel,
        grid=(n_chunks,),
        in_specs=[pl.BlockSpec(ublk, lambda i: (bwd_idx(i), 0)),
                  pl.BlockSpec((SSM_CHUNK, LANES), lambda i: (bwd_idx(i), 0)),
                  pl.BlockSpec((8, LANES), lambda i: (0, 0)),
                  pl.BlockSpec((SSM_CHUNK, BRANCH_W), lambda i: (bwd_idx(i), 0)),
                  pl.BlockSpec((SSM_CHUNK, zw), lambda i: (bwd_idx(i), z0)),
                  pl.BlockSpec((SSM_CHUNK, zw), lambda i: (bwd_idx(i), z0 + 1)),
                  pl.BlockSpec((1, BRANCH_W), lambda i: (0, 0)),
                  pl.BlockSpec((1, BRANCH_W), lambda i: (0, 0))],
        out_specs=pl.BlockSpec((SSM_CHUNK, BRANCH_W), lambda i: (bwd_idx(i), 0)),
        out_shape=jax.ShapeDtypeStruct((r, BRANCH_W), BF16),
        scratch_shapes=[state, pltpu.VMEM((SSM_CHUNK, BRANCH_W), F32)],
        compiler_params=_params(("arbitrary",), 32),
        name="ssd_bwd",
    )(u, dt_raw, cst, yf, p, p, d_exp, ssm_norm_w.reshape(1, BRANCH_W))


def _qkprep_kernel(qk_ref, cos_ref, sin_ref, qw_ref, kw_ref, q_ref, k_ref):
    cosv = cos_ref[...]
    sinv = sin_ref[...]
    scale = HEAD_DIM ** -0.5
    for h in range(ATT_HEADS + KV_HEADS):
        t = qk_ref[:, h * HEAD_DIM:(h + 1) * HEAD_DIM].astype(F32)
        w = qw_ref[...] if h < ATT_HEADS else kw_ref[...]
        n = t * lax.rsqrt(jnp.mean(t * t, axis=-1, keepdims=True) + EPS) * w
        rot = n * cosv + pltpu.roll(n, HEAD_DIM // 2, 1) * sinv
        if h < ATT_HEADS:
            q_ref[:, h * HEAD_DIM:(h + 1) * HEAD_DIM] = (rot * scale).astype(q_ref.dtype)
        else:
            hk = h - ATT_HEADS
            k_ref[:, hk * HEAD_DIM:(hk + 1) * HEAD_DIM] = rot.astype(k_ref.dtype)


def qk_prep(p, cos_t, sin_t, qw, kw):
    r = p.shape[0]
    tr = 256
    kvw = KV_HEADS * HEAD_DIM
    return pl.pallas_call(
        _qkprep_kernel,
        grid=(r // tr,),
        in_specs=[pl.BlockSpec((tr, QK_W), lambda i: (i, OFF_QA // QK_W)),
                  pl.BlockSpec((tr, HEAD_DIM), lambda i: (i, 0)),
                  pl.BlockSpec((tr, HEAD_DIM), lambda i: (i, 0)),
                  pl.BlockSpec((1, HEAD_DIM), lambda i: (0, 0)),
                  pl.BlockSpec((1, HEAD_DIM), lambda i: (0, 0))],
        out_specs=[pl.BlockSpec((tr, BRANCH_W), lambda i: (i, 0)),
                   pl.BlockSpec((tr, kvw), lambda i: (i, 0))],
        out_shape=[jax.ShapeDtypeStruct((r, BRANCH_W), BF16),
                   jax.ShapeDtypeStruct((r, kvw), BF16)],
        compiler_params=_params(("arbitrary",), 32),
        name="qk_prep",
    )(p, cos_t, sin_t, qw, kw)


def _flash_kernel(q_ref, k_ref, v_ref, g_ref, o_ref, m_ref, l_ref, acc_ref, *, nk):
    ki = pl.program_id(2)

    @pl.when(ki == 0)
    def _():
        m_ref[...] = jnp.full(m_ref.shape, -jnp.inf, F32)
        l_ref[...] = jnp.zeros(l_ref.shape, F32)
        acc_ref[...] = jnp.zeros(acc_ref.shape, F32)

    k = k_ref[...]
    v = v_ref[...]
    reps = k.shape[0] // LANES
    for r in range(ATT_GROUP):
        hs = slice(r * HEAD_DIM, (r + 1) * HEAD_DIM)
        s = lax.dot_general(q_ref[:, hs], k, (((1,), (1,)), ((), ())), preferred_element_type=F32)
        m_prev = m_ref[r]
        m_next = jnp.maximum(m_prev, jnp.max(s, axis=1, keepdims=True))
        p = jnp.exp(s - pltpu.repeat(m_next, reps, 1))
        alpha = jnp.exp(m_prev - m_next)
        l_ref[r] = alpha * l_ref[r] + jnp.sum(p, axis=1, keepdims=True)
        acc_ref[r] = alpha * acc_ref[r] + jnp.dot(p.astype(v.dtype), v, preferred_element_type=F32)
        m_ref[r] = m_next

    @pl.when(ki == nk - 1)
    def _():
        for r in range(ATT_GROUP):
            hs = slice(r * HEAD_DIM, (r + 1) * HEAD_DIM)
            o = acc_ref[r] / l_ref[r]
            o_ref[:, hs] = (_silu(g_ref[:, hs].astype(F32)) * o).astype(o_ref.dtype)


def flash_gqa(qn, kn, p, *, q_rows, q_row0, k_rows, k_row0, tq, tk):
    nq = q_rows // tq
    nk = k_rows // tk
    qb0 = q_row0 // tq
    kb0 = k_row0 // tk
    gw = ATT_GROUP * HEAD_DIM
    return pl.pallas_call(
        functools.partial(_flash_kernel, nk=nk),
        grid=(KV_HEADS, nq, nk),
        in_specs=[pl.BlockSpec((tq, gw), lambda g, i, j: (qb0 + i, g)),
                  pl.BlockSpec((tk, HEAD_DIM), lambda g, i, j: (kb0 + j, g)),
                  pl.BlockSpec((tk, HEAD_DIM), lambda g, i, j: (kb0 + j, OFF_VA // HEAD_DIM + g)),
                  pl.BlockSpec((tq, gw), lambda g, i, j: (qb0 + i, OFF_GA // gw + g))],
        out_specs=pl.BlockSpec((tq, gw), lambda g, i, j: (i, g)),
        out_shape=jax.ShapeDtypeStruct((q_rows, BRANCH_W), BF16),
        scratch_shapes=[pltpu.VMEM((ATT_GROUP, tq, LANES), F32),
                        pltpu.VMEM((ATT_GROUP, tq, LANES), F32),
                        pltpu.VMEM((ATT_GROUP, tq, HEAD_DIM), F32)],
        compiler_params=_params(("arbitrary", "arbitrary", "arbitrary"), 48),
        name="flash_gqa",
    )(qn, kn, p, p)


def _na_kernel(q_ref, k_ref, v_ref, g_ref, tp_ref, o_ref, *, n_lat, n_ctx, grid_rows):
    scale = HEAD_DIM ** -0.5
    win = NA_WIN_R * GRID_W
    kc = k_ref[n_lat:n_lat + n_ctx, :]
    vc = v_ref[n_lat:n_lat + n_ctx, :]
    nt = (((1,), (1,)), ((), ()))

    def row_body(r, carry):
        r0 = jnp.clip(r - NA_WIN_R // 2, 0, grid_rows - NA_WIN_R)
        d = r0 - r + (NA_WIN_R - 1)
        qs = pl.ds(pl.multiple_of(r * GRID_W, GRID_W), GRID_W)
        ws = pl.ds(pl.multiple_of(r0 * GRID_W, GRID_W), win)
        q = q_ref[qs, :]
        bias = jnp.concatenate([tp_ref[d + 2 * x] for x in range(NA_WIN_R // 2)], axis=1)
        s_loc = lax.dot_general(q, k_ref[ws, :], nt, preferred_element_type=F32) * scale + bias
        s_ctx = lax.dot_general(q, kc, nt, preferred_element_type=F32) * scale
        m = jnp.maximum(jnp.max(s_loc, axis=1, keepdims=True), jnp.max(s_ctx, axis=1, keepdims=True))
        p_loc = jnp.exp(s_loc - m)
        p_ctx = jnp.exp(s_ctx - m)
        den = jnp.sum(p_loc, axis=1, keepdims=True) + jnp.sum(p_ctx, axis=1, keepdims=True)
        o = (jnp.dot(p_loc.astype(BF16), v_ref[ws, :], preferred_element_type=F32)
             + jnp.dot(p_ctx.astype(BF16), vc, preferred_element_type=F32)) / den
        o_ref[qs, :] = (_silu(g_ref[qs, :].astype(F32)) * o).astype(o_ref.dtype)
        return carry

    lax.fori_loop(0, grid_rows, row_body, 0)

    qs = slice(n_lat, n_lat + n_ctx)
    s = lax.dot_general(q_ref[qs, :], kc, nt, preferred_element_type=F32) * scale
    pc = jnp.exp(s - jnp.max(s, axis=1, keepdims=True))
    oc = jnp.dot(pc.astype(BF16), vc, preferred_element_type=F32) / jnp.sum(pc, axis=1, keepdims=True)
    o_ref[qs, :] = (_silu(g_ref[qs, :].astype(F32)) * oc).astype(o_ref.dtype)


def neighbourhood(p, tp, n_lat):
    r = p.shape[0]
    n_ctx = r - n_lat
    col = lambda off: (lambda h: (0, off // HEAD_DIM + h))
    return pl.pallas_call(
        functools.partial(_na_kernel, n_lat=n_lat, n_ctx=n_ctx, grid_rows=n_lat // GRID_W),
        grid=(NA_HEADS,),
        in_specs=[pl.BlockSpec((r, HEAD_DIM), col(OFF_QN)),
                  pl.BlockSpec((r, HEAD_DIM), col(OFF_KN)),
                  pl.BlockSpec((r, HEAD_DIM), col(OFF_VN)),
                  pl.BlockSpec((r, HEAD_DIM), col(OFF_GN)),
                  pl.BlockSpec((None, 2 * NA_WIN_R - 2, GRID_W, 2 * GRID_W), lambda h: (h, 0, 0, 0))],
        out_specs=pl.BlockSpec((r, HEAD_DIM), lambda h: (0, h)),
        out_shape=jax.ShapeDtypeStruct((r, BRANCH_W), BF16),
        compiler_params=_params(("arbitrary",), 48),
        name="neighbourhood",
    )(p, p, p, p, tp)


def na_bias_table(rpb):
    c = np.arange(GRID_W)[:, None]
    kc = np.arange(GRID_W)[None, :]
    c0 = np.clip(c - NA_WIN_C // 2, 0, GRID_W - NA_WIN_C)
    valid = (kc >= c0) & (kc < c0 + NA_WIN_C)
    idx = np.clip(kc - c + (NA_WIN_C - 1), 0, 2 * NA_WIN_C - 2)
    t = jnp.where(valid, rpb.astype(F32)[:, :, idx], NEG_BIAS)
    return jnp.concatenate([t[:, :-1], t[:, 1:]], axis=-1)


def _gateup_kernel(a_ref, b_ref, c_ref, wa_ref, wb_ref, wc_ref, ga_ref, gb_ref, gc_ref, o_ref):
    acc = None
    for br_ref, w_ref, g_ref in ((a_ref, wa_ref, ga_ref), (b_ref, wb_ref, gb_ref), (c_ref, wc_ref, gc_ref)):
        up = jnp.dot(br_ref[...], w_ref[...], preferred_element_type=F32)
        term = _sigmoid(g_ref[...].astype(F32)) * up
        acc = term if acc is None else acc + term
    o_ref[...] = acc.astype(o_ref.dtype)


def gate_up(br_a, br_b, br_c, w_up, p, *, rows, row0s, p_row0):
    tm = _pick(rows, (1024, 512, 256))
    tn = 512
    nj = D_MODEL // tn

    def br_spec(row0):
        return pl.BlockSpec((tm, BRANCH_W), lambda i, j: (row0 // tm + i, 0))

    def w_spec(b):
        return pl.BlockSpec((None, BRANCH_W, tn), lambda i, j: (b, 0, j))

    def g_spec(b):
        return pl.BlockSpec((tm, tn), lambda i, j: (p_row0 // tm + i, (OFF_GM + b * D_MODEL) // tn + j))

    return pl.pallas_call(
        _gateup_kernel,
        grid=(rows // tm, nj),
        in_specs=[br_spec(row0s[0]), br_spec(row0s[1]), br_spec(row0s[2]),
                  w_spec(0), w_spec(1), w_spec(2), g_spec(0), g_spec(1), g_spec(2)],
        out_specs=pl.BlockSpec((tm, tn), lambda i, j: (i, j)),
        out_shape=jax.ShapeDtypeStruct((rows, D_MODEL), BF16),
        compiler_params=_params(("arbitrary", "arbitrary"), 56),
        name="gate_up",
    )(br_a, br_b, br_c, w_up, w_up, w_up, p, p, p)


def _outproj_kernel(s_ref, w_ref, x_ref, mod_ref, o_ref, *, mod_row):
    y = jnp.dot(s_ref[...], w_ref[...], preferred_element_type=F32)
    o_ref[...] = x_ref[...] + mod_ref[mod_row:mod_row + 1, :] * y


def out_proj(s, w_out, x, mod, mod_row):
    rows = s.shape[0]
    tm = _pick(rows, (1024, 512, 256))
    tn = 512
    return pl.pallas_call(
        functools.partial(_outproj_kernel, mod_row=mod_row),
        grid=(rows // tm, D_MODEL // tn),
        in_specs=[pl.BlockSpec((tm, D_MODEL), lambda i, j: (i, 0)),
                  pl.BlockSpec((D_MODEL, tn), lambda i, j: (0, j)),
                  pl.BlockSpec((tm, tn), lambda i, j: (i, j)),
                  pl.BlockSpec((8, tn), lambda i, j: (0, 2 * D_MODEL // tn + j))],
        out_specs=pl.BlockSpec((tm, tn), lambda i, j: (i, j)),
        out_shape=jax.ShapeDtypeStruct((rows, D_MODEL), F32),
        compiler_params=_params(("arbitrary", "arbitrary"), 48),
        name="out_proj",
    )(s, w_out, x, mod)


def _final_norm_kernel(x_ref, w_ref, o_ref):
    xv = x_ref[...]
    o_ref[...] = xv * lax.rsqrt(jnp.mean(xv * xv, axis=-1, keepdims=True) + EPS) * w_ref[...]


def final_norm(x, w):
    t, d = x.shape
    tr = 256
    return pl.pallas_call(
        _final_norm_kernel,
        grid=(t // tr,),
        in_specs=[pl.BlockSpec((tr, d), lambda i: (i, 0)), pl.BlockSpec((1, d), lambda i: (0, 0))],
        out_specs=pl.BlockSpec((tr, d), lambda i: (i, 0)),
        out_shape=jax.ShapeDtypeStruct((t, d), F32),
        compiler_params=_params(("arbitrary",), 40),
        name="final_norm",
    )(x, w.reshape(1, d))


def _deinterleave_cols(w):
    lead = w.shape[:-1]
    heads = w.shape[-1] // HEAD_DIM
    w4 = w.reshape(*lead, heads, HEAD_DIM // 2, 2)
    return jnp.swapaxes(w4, -1, -2).reshape(*lead, heads * HEAD_DIM)


def _relayout_in_proj(w_in, b_in):
    def split(a):
        main = jnp.concatenate([a[..., :_SRC_DT],
                                _deinterleave_cols(a[..., _SRC_QA:_SRC_VA]),
                                a[..., _SRC_VA:]], axis=-1)
        dt = jnp.pad(a[..., _SRC_DT:_SRC_QA], [(0, 0)] * (a.ndim - 1) + [(0, LANES - 2 * SSM_HEADS)])
        return main, dt

    w_main, w_dt = split(w_in)
    b_main, b_dt = split(b_in)
    return w_main.astype(BF16), b_main, w_dt.astype(BF16), b_dt


def _rope_tables(n_lat, n_ctx):
    pos = jnp.arange(n_lat, dtype=jnp.int32)
    row = (pos // GRID_W).astype(F32)
    colp = (pos % GRID_W).astype(F32)
    n_freq = HEAD_DIM // 4
    inv_freq = ROPE_THETA ** (-jnp.arange(n_freq, dtype=F32) / n_freq)
    ang = jnp.concatenate([row[:, None] * inv_freq, colp[:, None] * inv_freq], axis=-1)
    cos, sin = jnp.cos(ang), jnp.sin(ang)
    cos_t = jnp.concatenate([cos, cos], axis=-1)
    sin_t = jnp.concatenate([-sin, sin], axis=-1)
    cos_t = jnp.concatenate([cos_t, jnp.ones((n_ctx, HEAD_DIM), F32)], axis=0)
    sin_t = jnp.concatenate([sin_t, jnp.zeros((n_ctx, HEAD_DIM), F32)], axis=0)
    return cos_t, sin_t


def hybrid_layer(x, ctx, sb, lp, cos_t, sin_t, update_ctx):
    t = x.shape[0]
    lc = ctx.shape[0]
    mod = modulation(sb, lp["w_mod"], lp["b_mod"])
    h = norm_modulate(x, ctx, lp["norm_w"], mod)
    p = matmul_bias(h, lp["w_main"], lp["b_main"], BF16, "in_proj")
    dt_raw = matmul_bias(h, lp["w_dt"], lp["b_dt"], F32, "dt_proj")

    u = ssd_conv(p, lp["conv_w"], lp["conv_b"], t // SSM_CHUNK)
    br_a = ssd_branch(u, dt_raw, lp["ssd_cst"], p, lp["d_exp"], lp["ssm_norm_w"], lc // SSM_CHUNK)

    qn, kn = qk_prep(p, cos_t, sin_t, lp["q_norm_w"], lp["k_norm_w"])
    tq = _pick(t, (1024, 512, 256))
    tk = _pick(t + lc, (768, 640, 512, 384, 256))
    br_b = flash_gqa(qn, kn, p, q_rows=t, q_row0=0, k_rows=t + lc, k_row0=0, tq=tq, tk=tk)

    br_c = neighbourhood(p, lp["na_bias"], t)

    s = gate_up(br_a, br_b, br_c, lp["w_up"], p, rows=t, row0s=(0, 0, 0), p_row0=0)
    x_new = out_proj(s, lp["w_out"], x, mod, 0)
    if not update_ctx:
        return x_new, ctx
    br_b_c = flash_gqa(qn, kn, p, q_rows=lc, q_row0=t, k_rows=lc, k_row0=t, tq=lc, tk=lc)
    s_c = gate_up(br_a, br_b_c, br_c, lp["w_up"], p, rows=lc, row0s=(t, 0, t), p_row0=t)
    ctx_new = out_proj(s_c, lp["w_out"], ctx, mod, 1)
    return x_new, ctx_new


def kernel(x, c, ctx, c_ctx, norm_w, w_mod, b_mod, w_in, b_in, conv_w, conv_b, a_log, dt_bias, d_skip, ssm_norm_w, q_norm_w, k_norm_w, rpb, w_up, w_out, final_norm_w):
    xs = x[0]
    cs = ctx[0]
    t, lc = xs.shape[0], cs.shape[0]
    sb = jnp.broadcast_to(jnp.stack([c[0], c_ctx])[:, :, None], (2, D_MODEL, LANES))
    w_main, b_main, w_dt, b_dt = _relayout_in_proj(w_in, b_in)
    w_up_b = w_up.astype(BF16)
    w_out_b = w_out.astype(BF16)
    cos_t, sin_t = _rope_tables(t, lc)
    pad_heads = [(0, 0), (0, LANES - 2 * SSM_HEADS)]
    for layer in range(DEPTH):
        cst = jnp.concatenate([
            jnp.pad(dt_bias[layer].astype(F32).reshape(1, 2 * SSM_HEADS), pad_heads),
            jnp.pad(-jnp.exp(a_log[layer].astype(F32)).reshape(1, 2 * SSM_HEADS), pad_heads),
            jnp.zeros((6, LANES), F32)], axis=0)
        lp = dict(
            w_mod=w_mod[layer], b_mod=b_mod[layer], norm_w=norm_w[layer],
            w_main=w_main[layer], b_main=b_main[layer], w_dt=w_dt[layer], b_dt=b_dt[layer],
            conv_w=conv_w[layer], conv_b=conv_b[layer], ssd_cst=cst,
            d_exp=jnp.repeat(d_skip[layer].astype(F32), SSM_HEAD_DIM).reshape(1, BRANCH_W),
            ssm_norm_w=ssm_norm_w[layer],
            q_norm_w=_deinterleave_cols(q_norm_w[layer].reshape(1, HEAD_DIM)),
            k_norm_w=_deinterleave_cols(k_norm_w[layer].reshape(1, HEAD_DIM)),
            na_bias=na_bias_table(rpb[layer]),
            w_up=w_up_b[layer], w_out=w_out_b[layer])
        xs, cs = hybrid_layer(xs, cs, sb, lp, cos_t, sin_t, layer < DEPTH - 1)
    return final_norm(xs, final_norm_w)[None]
```

```python
import functools
import math

import numpy as np
import jax
import jax.numpy as jnp
from jax import lax
from jax.experimental import pallas as pl
from jax.experimental.pallas import tpu as pltpu

F32 = jnp.float32
BF16 = jnp.bfloat16

D_MODEL = 4096
DEPTH = 2
GRID_W = 64
N_BRANCH = 3
BRANCH_W = 2048
EPS = 1e-6

SSM_HEADS = 32
SSM_HEAD_DIM = 64
SSM_GROUPS = 4
SSM_STATE = 128
SSM_CHUNK = 128
CONV_K = 5
SSM_GN = SSM_GROUPS * SSM_STATE
XBC_W = BRANCH_W + 2 * SSM_GN
DT_W = 2 * SSM_HEADS
GROUP_W = (SSM_HEADS // SSM_GROUPS) * SSM_HEAD_DIM

ATT_HEADS = 16
KV_HEADS = 4
ATT_GROUP = ATT_HEADS // KV_HEADS
HEAD_DIM = 128
ROPE_THETA = 10000.0

NA_HEADS = 16
NA_WIN_R = 8
NA_WIN_C = 16

LANES = 128
MIB = 1024 * 1024

_SRC_DT = XBC_W + BRANCH_W
_SRC_QA = _SRC_DT + 2 * SSM_HEADS
_SRC_VA = _SRC_QA + BRANCH_W + KV_HEADS * HEAD_DIM
OFF_XBC = 0
OFF_Z = OFF_XBC + XBC_W
OFF_QA = OFF_Z + BRANCH_W
OFF_KA = OFF_QA + BRANCH_W
OFF_VA = OFF_KA + KV_HEADS * HEAD_DIM
OFF_GA = OFF_VA + KV_HEADS * HEAD_DIM
OFF_QN = OFF_GA + BRANCH_W
OFF_KN = OFF_QN + BRANCH_W
OFF_VN = OFF_KN + BRANCH_W
OFF_GN = OFF_VN + BRANCH_W
OFF_GM = OFF_GN + BRANCH_W
P_COLS = OFF_GM + N_BRANCH * D_MODEL
QK_W = BRANCH_W + KV_HEADS * HEAD_DIM

NEG_BIAS = -1e30


def _params(sem, vmem_mib):
    return pltpu.CompilerParams(dimension_semantics=sem, vmem_limit_bytes=vmem_mib * MIB)


def _pick(n, candidates):
    for c in candidates:
        if n % c == 0:
            return c
    raise ValueError(f"no tile for {n} in {candidates}")


def _sigmoid(v):
    return 1.0 / (1.0 + jnp.exp(-v))


def _silu(v):
    return v * _sigmoid(v)


def _mod_kernel(s_ref, w_ref, b_ref, o_ref):
    tn = w_ref.shape[1]
    k = w_ref.shape[0]
    o_ref[...] = jnp.zeros(o_ref.shape, F32)
    for v in range(2):
        s = _silu(s_ref[v])
        for jb in range(tn // LANES):
            sl = slice(jb * LANES, (jb + 1) * LANES)
            prod = w_ref[:, sl] * s
            part = prod.reshape(k // 8, 8, LANES).sum(axis=0)
            o_ref[v:v + 1, sl] = part.sum(axis=0, keepdims=True) + b_ref[:, sl]


def modulation(sb, w_mod, b_mod, layer):
    _, k, n = w_mod.shape
    tn = 512
    return pl.pallas_call(
        _mod_kernel,
        grid=(n // tn,),
        in_specs=[pl.BlockSpec((2, k, LANES), lambda j: (0, 0, 0)),
                  pl.BlockSpec((None, k, tn), lambda j: (layer, 0, j)),
                  pl.BlockSpec((1, tn), lambda j: (0, j))],
        out_specs=pl.BlockSpec((8, tn), lambda j: (0, j)),
        out_shape=jax.ShapeDtypeStruct((8, n), F32),
        compiler_params=_params(("arbitrary",), 40),
        name="modulation",
    )(sb, w_mod, b_mod.reshape(1, n))


def _norm_kernel(x_ref, c_ref, nw_ref, mod_ref, o_ref, *, n_lat_blocks):
    i = pl.program_id(0)

    def emit(src_ref, row):
        xv = src_ref[...]
        y = xv * lax.rsqrt(jnp.mean(xv * xv, axis=-1, keepdims=True) + EPS) * nw_ref[...]
        shift = mod_ref[row:row + 1, 0:D_MODEL]
        scale = mod_ref[row:row + 1, D_MODEL:2 * D_MODEL]
        o_ref[...] = (y * (1.0 + scale) + shift).astype(o_ref.dtype)

    @pl.when(i < n_lat_blocks)
    def _():
        emit(x_ref, 0)

    @pl.when(i >= n_lat_blocks)
    def _():
        emit(c_ref, 1)


def norm_modulate(x, ctx, norm_w, mod):
    t, d = x.shape
    lc = ctx.shape[0]
    tr = 256
    n_lat = t // tr
    n_ctx = lc // tr
    return pl.pallas_call(
        functools.partial(_norm_kernel, n_lat_blocks=n_lat),
        grid=(n_lat + n_ctx,),
        in_specs=[pl.BlockSpec((tr, d), lambda i: (jnp.minimum(i, n_lat - 1), 0)),
                  pl.BlockSpec((tr, d), lambda i: (jnp.maximum(i - n_lat, 0), 0)),
                  pl.BlockSpec((1, d), lambda i: (0, 0)),
                  pl.BlockSpec((8, 3 * d), lambda i: (0, 0))],
        out_specs=pl.BlockSpec((tr, d), lambda i: (i, 0)),
        out_shape=jax.ShapeDtypeStruct((t + lc, d), BF16),
        compiler_params=_params(("arbitrary",), 40),
        name="norm_modulate",
    )(x, ctx, norm_w.reshape(1, d), mod)


CAST_ROWS = 256


def _inproj_kernel(h_ref, wa_ref, wb_ref, b_ref, o_ref, w_ref, *, n_plain):
    j = pl.program_id(0)
    k = wa_ref.shape[0]
    tn = wa_ref.shape[1]

    @pl.when(jnp.logical_and(pl.program_id(1) == 0, j < n_plain))
    def _():
        for r0 in range(0, k, CAST_ROWS):
            w_ref[r0:r0 + CAST_ROWS, :] = wa_ref[r0:r0 + CAST_ROWS, :].astype(BF16)

    @pl.when(jnp.logical_and(pl.program_id(1) == 0, j >= n_plain))
    def _():
        for r0 in range(0, k, CAST_ROWS):
            both = jnp.concatenate([wa_ref[r0:r0 + CAST_ROWS, :], wb_ref[r0:r0 + CAST_ROWS, :]], axis=1)
            w_ref[r0:r0 + CAST_ROWS, :] = both[:, DT_W:DT_W + tn].astype(BF16)

    acc = jnp.dot(h_ref[...], w_ref[...], preferred_element_type=F32)
    o_ref[...] = (acc + b_ref[...]).astype(o_ref.dtype)


def in_proj(h, w_in, b_main, layer):
    m, k = h.shape
    tm = _pick(m, (1056, 640, 512, 256, 128))
    tn = 512
    n_plain = _SRC_DT // tn
    sub = tn // LANES
    return pl.pallas_call(
        functools.partial(_inproj_kernel, n_plain=n_plain),
        grid=(P_COLS // tn, m // tm),
        in_specs=[pl.BlockSpec((tm, k), lambda j, i: (i, 0)),
                  pl.BlockSpec((None, k, tn), lambda j, i: (layer, 0, j)),
                  pl.BlockSpec((None, k, LANES), lambda j, i: (layer, 0, (j + 1) * sub)),
                  pl.BlockSpec((1, tn), lambda j, i: (0, j))],
        out_specs=pl.BlockSpec((tm, tn), lambda j, i: (i, j)),
        out_shape=jax.ShapeDtypeStruct((m, P_COLS), BF16),
        scratch_shapes=[pltpu.VMEM((k, tn), BF16)],
        compiler_params=_params(("arbitrary", "arbitrary"), 56),
        name="in_proj",
    )(h, w_in, w_in, b_main.reshape(1, P_COLS))


def _dtproj_kernel(h_ref, w_ref, b_ref, o_ref):
    acc = jnp.dot(h_ref[...], w_ref[...].astype(BF16), preferred_element_type=F32)
    o_ref[...] = acc + b_ref[...]


def dt_proj(h, w_in, b_dt, layer):
    m, k = h.shape
    tm = _pick(m, (1408, 1280, 1024, 768, 640, 512, 256, 128))
    return pl.pallas_call(
        _dtproj_kernel,
        grid=(m // tm,),
        in_specs=[pl.BlockSpec((tm, k), lambda i: (i, 0)),
                  pl.BlockSpec((None, k, LANES), lambda i: (layer, 0, _SRC_DT // LANES)),
                  pl.BlockSpec((1, LANES), lambda i: (0, 0))],
        out_specs=pl.BlockSpec((tm, LANES), lambda i: (i, 0)),
        out_shape=jax.ShapeDtypeStruct((m, LANES), F32),
        compiler_params=_params(("arbitrary",), 48),
        name="dt_proj",
    )(h, w_in, b_dt.reshape(1, LANES))


def _conv_kernel(prev_ref, cur_ref, next_ref, w_ref, b_ref, o_ref, *, n_lat_chunks, n_chunks):
    c = pl.program_id(0)
    has_prev = jnp.logical_and(c != 0, c != n_lat_chunks)
    has_next = jnp.logical_and(c != n_lat_chunks - 1, c != n_chunks - 1)
    pad = CONV_K // 2
    cw = 512
    for jb in range(XBC_W // cw):
        sl = slice(jb * cw, (jb + 1) * cw)
        cur = cur_ref[:, sl].astype(F32)
        pv = prev_ref[SSM_CHUNK - 16:SSM_CHUNK, sl].astype(F32)[8:16]
        nx = next_ref[0:16, sl].astype(F32)[0:8]
        pv = jnp.where(has_prev, pv, 0.0)
        nx = jnp.where(has_next, nx, 0.0)
        ext = jnp.concatenate([pv, cur, nx], axis=0)
        acc = jnp.broadcast_to(b_ref[:, sl], (SSM_CHUNK, cw))
        for kk in range(CONV_K):
            lo = 8 + kk - pad
            acc = acc + w_ref[kk:kk + 1, sl] * ext[lo:lo + SSM_CHUNK]
        o_ref[:, sl] = _silu(acc).astype(o_ref.dtype)


def ssd_conv(p, conv_w, conv_b, n_lat_chunks):
    r = p.shape[0]
    n_chunks = r // SSM_CHUNK

    def prev_map(c):
        bad = jnp.logical_or(c == 0, c == n_lat_chunks)
        return (jnp.where(bad, c, c - 1), 0)

    def next_map(c):
        bad = jnp.logical_or(c == n_lat_chunks - 1, c == n_chunks - 1)
        return (jnp.where(bad, c, c + 1), 0)

    blk = (SSM_CHUNK, XBC_W)
    return pl.pallas_call(
        functools.partial(_conv_kernel, n_lat_chunks=n_lat_chunks, n_chunks=n_chunks),
        grid=(n_chunks,),
        in_specs=[pl.BlockSpec(blk, prev_map),
                  pl.BlockSpec(blk, lambda c: (c, 0)),
                  pl.BlockSpec(blk, next_map),
                  pl.BlockSpec((8, XBC_W), lambda c: (0, 0)),
                  pl.BlockSpec((1, XBC_W), lambda c: (0, 0))],
        out_specs=pl.BlockSpec(blk, lambda c: (c, 0)),
        out_shape=jax.ShapeDtypeStruct((r, XBC_W), BF16),
        compiler_params=_params(("arbitrary",), 32),
        name="ssd_conv",
    )(p, p, p, jnp.pad(conv_w, ((0, 8 - CONV_K), (0, 0))), conv_b.reshape(1, XBC_W))


def _dot3(tri, v):
    hi = v.astype(BF16)
    r1 = v - hi.astype(F32)
    mid = r1.astype(BF16)
    lo = (r1 - mid.astype(F32)).astype(BF16)
    return (jnp.dot(tri, hi, preferred_element_type=F32)
            + jnp.dot(tri, mid, preferred_element_type=F32)
            + jnp.dot(tri, lo, preferred_element_type=F32))


def _pair_cols(v, ha, lane_lo):
    rows = v.shape[0]
    a = jnp.broadcast_to(v[:, ha:ha + 1], (rows, LANES))
    b = jnp.broadcast_to(v[:, ha + 1:ha + 2], (rows, LANES))
    return jnp.where(lane_lo, a, b)


def _ssd_chunk(u_ref, dt_ref, cst_ref, ht_ref, emit, *, reverse):
    L = SSM_CHUNK
    hoff = SSM_HEADS if reverse else 0
    li = lax.broadcasted_iota(jnp.int32, (L, L), 0)
    si = lax.broadcasted_iota(jnp.int32, (L, L), 1)
    keep = (li <= si) if reverse else (li >= si)
    tri = jnp.where((si >= li) if reverse else (si <= li), 1.0, 0.0).astype(BF16)
    lane_lo = lax.broadcasted_iota(jnp.int32, (L, LANES), 1) < SSM_HEAD_DIM
    lane_lo1 = lax.broadcasted_iota(jnp.int32, (1, LANES), 1) < SSM_HEAD_DIM

    raw = dt_ref[...] + cst_ref[0:1, :]
    dtv = jnp.maximum(raw, 0.0) + jnp.log1p(jnp.exp(-jnp.abs(raw)))
    da = dtv * cst_ref[1:2, :]
    acs = _dot3(tri, da)
    acs_t = acs.T
    dt_t = dtv.T
    a_last = acs[0:1, :] if reverse else acs[L - 1:L, :]
    wst = jnp.exp(a_last - acs) * dtv
    eacs = jnp.exp(acs)
    e_last = jnp.exp(a_last)

    pairs = GROUP_W // LANES
    for g in range(SSM_GROUPS):
        bg = u_ref[:, BRANCH_W + g * SSM_STATE:BRANCH_W + (g + 1) * SSM_STATE]
        cg = u_ref[:, BRANCH_W + SSM_GN + g * SSM_STATE:BRANCH_W + SSM_GN + (g + 1) * SSM_STATE]
        cb = lax.dot_general(cg, bg, (((1,), (1,)), ((), ())), preferred_element_type=F32)
        bg_t = bg.astype(F32).T.astype(BF16)
        y_parts, xw_parts, e_parts, d_parts = [], [], [], []
        for pi in range(pairs):
            ha = hoff + g * (2 * pairs) + 2 * pi
            xp = u_ref[:, g * GROUP_W + pi * LANES:g * GROUP_W + (pi + 1) * LANES]
            zero = jnp.zeros_like(xp)
            y_pair = None
            for hh, xm in ((ha, jnp.where(lane_lo, xp, zero)), (ha + 1, jnp.where(lane_lo, zero, xp))):
                diff = acs[:, hh:hh + 1] - acs_t[hh:hh + 1, :]
                dec = jnp.exp(jnp.where(keep, diff, -jnp.inf))
                m = (cb * dec * dt_t[hh:hh + 1, :]).astype(BF16)
                yh = jnp.dot(m, xm, preferred_element_type=F32)
                y_pair = yh if y_pair is None else y_pair + yh
            y_parts.append(y_pair)
            xw_parts.append((xp.astype(F32) * _pair_cols(wst, ha, lane_lo)).astype(BF16))
            e_parts.append(_pair_cols(eacs, ha, lane_lo))
            d_parts.append(_pair_cols(e_last, ha, lane_lo1))
        hs = ht_ref[g]
        y_off = jnp.dot(cg, hs.astype(BF16), preferred_element_type=F32) * jnp.concatenate(e_parts, axis=1)
        s_new = jnp.dot(bg_t, jnp.concatenate(xw_parts, axis=1), preferred_element_type=F32)
        ht_ref[g] = hs * jnp.concatenate(d_parts, axis=1) + s_new
        emit(g, jnp.concatenate(y_parts, axis=1) + y_off)


def _ssd_fwd_kernel(u_ref, dt_ref, cst_ref, o_ref, ht_ref):
    @pl.when(pl.program_id(0) == 0)
    def _():
        ht_ref[...] = jnp.zeros(ht_ref.shape, F32)

    def emit(g, y):
        o_ref[:, g * GROUP_W:(g + 1) * GROUP_W] = y

    _ssd_chunk(u_ref, dt_ref, cst_ref, ht_ref, emit, reverse=False)


def _ssd_bwd_kernel(u_ref, dt_ref, cst_ref, yf_ref, z0_ref, z1_ref, dsk_ref, nw_ref, o_ref, ht_ref, y_ref):
    @pl.when(pl.program_id(0) == 0)
    def _():
        ht_ref[...] = jnp.zeros(ht_ref.shape, F32)

    half = SSM_GROUPS // 2

    def emit(g, y):
        sl = slice(g * GROUP_W, (g + 1) * GROUP_W)
        z_ref = z0_ref if g < half else z1_ref
        zsl = slice((g % half) * GROUP_W, (g % half + 1) * GROUP_W)
        tot = y + yf_ref[:, sl] + dsk_ref[:, sl] * u_ref[:, sl].astype(F32)
        y_ref[:, sl] = tot * _silu(z_ref[:, zsl].astype(F32))

    _ssd_chunk(u_ref, dt_ref, cst_ref, ht_ref, emit, reverse=True)
    yv = y_ref[...]
    o_ref[...] = (yv * lax.rsqrt(jnp.mean(yv * yv, axis=-1, keepdims=True) + EPS) * nw_ref[...]).astype(o_ref.dtype)


def ssd_branch(u, dt_raw, cst, p, d_exp, ssm_norm_w, n_ctx_chunks):
    r = u.shape[0]
    n_chunks = r // SSM_CHUNK
    n_lat = n_chunks - n_ctx_chunks
    state = pltpu.VMEM((SSM_GROUPS, SSM_STATE, GROUP_W), F32)

    def fwd_idx(i):
        return jnp.where(i < n_ctx_chunks, n_lat + i, i - n_ctx_chunks)

    def bwd_idx(i):
        return n_chunks - 1 - i

    ublk = (SSM_CHUNK, XBC_W)
    yf = pl.pallas_call(
        _ssd_fwd_kernel,
        grid=(n_chunks,),
        in_specs=[pl.BlockSpec(ublk, lambda i: (fwd_idx(i), 0)),
                  pl.BlockSpec((SSM_CHUNK, LANES), lambda i: (fwd_idx(i), 0)),
                  pl.BlockSpec((8, LANES), lambda i: (0, 0))],
        out_specs=pl.BlockSpec((SSM_CHUNK, BRANCH_W), lambda i: (fwd_idx(i), 0)),
        out_shape=jax.ShapeDtypeStruct((r, BRANCH_W), F32),
        scratch_shapes=[state],
        compiler_params=_params(("arbitrary",), 32),
        name="ssd_fwd",
    )(u, dt_raw, cst)

    zw = BRANCH_W // 2
    z0 = OFF_Z // zw
    return pl.pallas_call(
        _ssd_bwd_kernel,
        grid=(n_chunks,),
        in_specs=[pl.BlockSpec(ublk, lambda i: (bwd_idx(i), 0)),
                  pl.BlockSpec((SSM_CHUNK, LANES), lambda i: (bwd_idx(i), 0)),
                  pl.BlockSpec((8, LANES), lambda i: (0, 0)),
                  pl.BlockSpec((SSM_CHUNK, BRANCH_W), lambda i: (bwd_idx(i), 0)),
                  pl.BlockSpec((SSM_CHUNK, zw), lambda i: (bwd_idx(i), z0)),
                  pl.BlockSpec((SSM_CHUNK, zw), lambda i: (bwd_idx(i), z0 + 1)),
                  pl.BlockSpec((1, BRANCH_W), lambda i: (0, 0)),
                  pl.BlockSpec((1, BRANCH_W), lambda i: (0, 0))],
        out_specs=pl.BlockSpec((SSM_CHUNK, BRANCH_W), lambda i: (bwd_idx(i), 0)),
        out_shape=jax.ShapeDtypeStruct((r, BRANCH_W), BF16),
        scratch_shapes=[state, pltpu.VMEM((SSM_CHUNK, BRANCH_W), F32)],
        compiler_params=_params(("arbitrary",), 32),
        name="ssd_bwd",
    )(u, dt_raw, cst, yf, p, p, d_exp, ssm_norm_w.reshape(1, BRANCH_W))


def _qkprep_kernel(qk_ref, cos_ref, sin_ref, qw_ref, kw_ref, q_ref, k_ref):
    cosv = cos_ref[...]
    sinv = sin_ref[...]
    scale = HEAD_DIM ** -0.5 * math.log2(math.e)
    even = lax.broadcasted_iota(jnp.int32, cosv.shape, 1) % 2 == 0
    for h in range(ATT_HEADS + KV_HEADS):
        t = qk_ref[:, h * HEAD_DIM:(h + 1) * HEAD_DIM].astype(F32)
        w = qw_ref[...] if h < ATT_HEADS else kw_ref[...]
        n = t * lax.rsqrt(jnp.mean(t * t, axis=-1, keepdims=True) + EPS) * w
        partner = jnp.where(even, pltpu.roll(n, HEAD_DIM - 1, 1), pltpu.roll(n, 1, 1))
        rot = n * cosv + partner * sinv
        if h < ATT_HEADS:
            q_ref[:, h * HEAD_DIM:(h + 1) * HEAD_DIM] = (rot * scale).astype(q_ref.dtype)
        else:
            hk = h - ATT_HEADS
            k_ref[:, hk * HEAD_DIM:(hk + 1) * HEAD_DIM] = rot.astype(k_ref.dtype)


def qk_prep(p, cos_t, sin_t, qw, kw):
    r = p.shape[0]
    tr = 256
    kvw = KV_HEADS * HEAD_DIM
    return pl.pallas_call(
        _qkprep_kernel,
        grid=(r // tr,),
        in_specs=[pl.BlockSpec((tr, QK_W), lambda i: (i, OFF_QA // QK_W)),
                  pl.BlockSpec((tr, HEAD_DIM), lambda i: (i, 0)),
                  pl.BlockSpec((tr, HEAD_DIM), lambda i: (i, 0)),
                  pl.BlockSpec((1, HEAD_DIM), lambda i: (0, 0)),
                  pl.BlockSpec((1, HEAD_DIM), lambda i: (0, 0))],
        out_specs=[pl.BlockSpec((tr, BRANCH_W), lambda i: (i, 0)),
                   pl.BlockSpec((tr, kvw), lambda i: (i, 0))],
        out_shape=[jax.ShapeDtypeStruct((r, BRANCH_W), BF16),
                   jax.ShapeDtypeStruct((r, kvw), BF16)],
        compiler_params=_params(("arbitrary",), 32),
        name="qk_prep",
    )(p, cos_t, sin_t, qw, kw)


def _flash_kernel(q_ref, k_ref, v_ref, g_ref, o_ref, m_ref, l_ref, acc_ref, *, nk):
    ki = pl.program_id(2)

    @pl.when(ki == 0)
    def _():
        m_ref[...] = jnp.full(m_ref.shape, -jnp.inf, F32)
        l_ref[...] = jnp.zeros(l_ref.shape, F32)
        acc_ref[...] = jnp.zeros(acc_ref.shape, F32)

    k = k_ref[...]
    v1 = jnp.concatenate([v_ref[...], jnp.ones(v_ref.shape, v_ref.dtype)], axis=1)
    reps = k.shape[0] // LANES
    for r in range(ATT_GROUP):
        hs = slice(r * HEAD_DIM, (r + 1) * HEAD_DIM)
        s = lax.dot_general(q_ref[:, hs], k, (((1,), (1,)), ((), ())), preferred_element_type=F32)
        m_prev = m_ref[r]
        m_next = jnp.maximum(m_prev, jnp.max(s, axis=1, keepdims=True))
        p = jnp.exp2(s - pltpu.repeat(m_next, reps, 1))
        alpha = jnp.exp2(m_prev - m_next)
        pv = jnp.dot(p.astype(v1.dtype), v1, preferred_element_type=F32)
        l_ref[r] = alpha * l_ref[r] + pv[:, HEAD_DIM:]
        acc_ref[r] = alpha * acc_ref[r] + pv[:, :HEAD_DIM]
        m_ref[r] = m_next

    @pl.when(ki == nk - 1)
    def _():
        for r in range(ATT_GROUP):
            hs = slice(r * HEAD_DIM, (r + 1) * HEAD_DIM)
            o = acc_ref[r] / l_ref[r]
            o_ref[:, hs] = (_silu(g_ref[:, hs].astype(F32)) * o).astype(o_ref.dtype)


def flash_gqa(qn, kn, p, *, q_rows, q_row0, k_rows, k_row0, tq, tk):
    nq = q_rows // tq
    nk = k_rows // tk
    qb0 = q_row0 // tq
    kb0 = k_row0 // tk
    gw = ATT_GROUP * HEAD_DIM
    return pl.pallas_call(
        functools.partial(_flash_kernel, nk=nk),
        grid=(KV_HEADS, nq, nk),
        in_specs=[pl.BlockSpec((tq, gw), lambda g, i, j: (qb0 + i, g)),
                  pl.BlockSpec((tk, HEAD_DIM), lambda g, i, j: (kb0 + j, g)),
                  pl.BlockSpec((tk, HEAD_DIM), lambda g, i, j: (kb0 + j, OFF_VA // HEAD_DIM + g)),
                  pl.BlockSpec((tq, gw), lambda g, i, j: (qb0 + i, OFF_GA // gw + g))],
        out_specs=pl.BlockSpec((tq, gw), lambda g, i, j: (i, g)),
        out_shape=jax.ShapeDtypeStruct((q_rows, BRANCH_W), BF16),
        scratch_shapes=[pltpu.VMEM((ATT_GROUP, tq, LANES), F32),
                        pltpu.VMEM((ATT_GROUP, tq, LANES), F32),
                        pltpu.VMEM((ATT_GROUP, tq, HEAD_DIM), F32)],
        compiler_params=_params(("arbitrary", "arbitrary", "arbitrary"), 48),
        name="flash_gqa",
    )(qn, kn, p, p)


NA_QROWS = 4
NA_KROWS = 12
NA_PATTERNS = 3


def _na_rel_row(pattern, qr, kr):
    if pattern == 0:
        return kr - qr + (NA_WIN_R - 1) if kr < NA_WIN_R else None
    if pattern == 1:
        return kr - qr + (NA_WIN_R - 1 - NA_WIN_R // 2) if 0 <= kr - qr < NA_WIN_R else None
    first = NA_KROWS - NA_WIN_R
    return kr - qr + (NA_WIN_R - 1) - (NA_KROWS - NA_QROWS) if kr >= first else None


def _na_kernel(q_ref, k_ref, v_ref, g_ref, tp_ref, o_ref, bias_ref, *, n_lat, n_ctx, grid_rows):
    scale = HEAD_DIM ** -0.5
    qn = NA_QROWS * GRID_W
    win = NA_KROWS * GRID_W
    n_blocks = grid_rows // NA_QROWS
    kc = k_ref[n_lat:n_lat + n_ctx, :]
    vc = v_ref[n_lat:n_lat + n_ctx, :]
    nt = (((1,), (1,)), ((), ()))

    lane_lo = lax.broadcasted_iota(jnp.int32, (GRID_W, LANES), 1) < GRID_W
    for pattern in range(NA_PATTERNS):
        for qr in range(NA_QROWS):
            for kp in range(NA_KROWS // 2):
                ja = _na_rel_row(pattern, qr, 2 * kp)
                jb = _na_rel_row(pattern, qr, 2 * kp + 1)
                if ja is None and jb is None:
                    tile = jnp.full((GRID_W, LANES), NEG_BIAS, F32)
                elif jb is None:
                    tile = jnp.where(lane_lo, tp_ref[ja + 1], NEG_BIAS)
                elif ja is None:
                    tile = jnp.where(lane_lo, NEG_BIAS, tp_ref[jb])
                else:
                    tile = tp_ref[jb]
                bias_ref[pattern, qr * GRID_W:(qr + 1) * GRID_W, kp * LANES:(kp + 1) * LANES] = tile

    def block_body(b, carry):
        u0 = jnp.clip(b * NA_QROWS - NA_WIN_R // 2, 0, grid_rows - NA_KROWS)
        pattern = jnp.where(b == 0, 0, jnp.where(b == n_blocks - 1, 2, 1))
        qs = pl.ds(pl.multiple_of(b * qn, qn), qn)
        ws = pl.ds(pl.multiple_of(u0 * GRID_W, GRID_W), win)
        q = q_ref[qs, :]
        s_loc = lax.dot_general(q, k_ref[ws, :], nt, preferred_element_type=F32) * scale + bias_ref[pattern]
        s_ctx = lax.dot_general(q, kc, nt, preferred_element_type=F32) * scale
        m = jnp.maximum(jnp.max(s_loc, axis=1, keepdims=True), jnp.max(s_ctx, axis=1, keepdims=True))
        p_loc = jnp.exp(s_loc - m)
        p_ctx = jnp.exp(s_ctx - m)
        den = jnp.sum(p_loc, axis=1, keepdims=True) + jnp.sum(p_ctx, axis=1, keepdims=True)
        o = (jnp.dot(p_loc.astype(BF16), v_ref[ws, :], preferred_element_type=F32)
             + jnp.dot(p_ctx.astype(BF16), vc, preferred_element_type=F32)) / den
        o_ref[qs, :] = (_silu(g_ref[qs, :].astype(F32)) * o).astype(o_ref.dtype)
        return carry

    lax.fori_loop(0, n_blocks, block_body, 0)

    qs = slice(n_lat, n_lat + n_ctx)
    s = lax.dot_general(q_ref[qs, :], kc, nt, preferred_element_type=F32) * scale
    pc = jnp.exp(s - jnp.max(s, axis=1, keepdims=True))
    oc = jnp.dot(pc.astype(BF16), vc, preferred_element_type=F32) / jnp.sum(pc, axis=1, keepdims=True)
    o_ref[qs, :] = (_silu(g_ref[qs, :].astype(F32)) * oc).astype(o_ref.dtype)


def neighbourhood(p, tp, n_lat):
    r = p.shape[0]
    n_ctx = r - n_lat
    col = lambda off: (lambda h: (0, off // HEAD_DIM + h))
    return pl.pallas_call(
        functools.partial(_na_kernel, n_lat=n_lat, n_ctx=n_ctx, grid_rows=n_lat // GRID_W),
        grid=(NA_HEADS,),
        in_specs=[pl.BlockSpec((r, HEAD_DIM), col(OFF_QN)),
                  pl.BlockSpec((r, HEAD_DIM), col(OFF_KN)),
                  pl.BlockSpec((r, HEAD_DIM), col(OFF_VN)),
                  pl.BlockSpec((r, HEAD_DIM), col(OFF_GN)),
                  pl.BlockSpec((None, 2 * NA_WIN_R, GRID_W, 2 * GRID_W), lambda h: (h, 0, 0, 0))],
        out_specs=pl.BlockSpec((r, HEAD_DIM), lambda h: (0, h)),
        out_shape=jax.ShapeDtypeStruct((r, BRANCH_W), BF16),
        scratch_shapes=[pltpu.VMEM((NA_PATTERNS, NA_QROWS * GRID_W, NA_KROWS * GRID_W), F32)],
        compiler_params=_params(("arbitrary",), 48),
        name="neighbourhood",
    )(p, p, p, p, tp)


def na_bias_table(rpb):
    n_rel = 2 * NA_WIN_C - 1
    c = np.arange(GRID_W)[:, None]
    kc = np.arange(GRID_W)[None, :]
    c0 = np.clip(c - NA_WIN_C // 2, 0, GRID_W - NA_WIN_C)
    valid = (kc >= c0) & (kc < c0 + NA_WIN_C)
    rel = kc - c + (NA_WIN_C - 1)
    onehot = (rel[None] == np.arange(n_rel)[:, None, None]) & valid[None]
    h, nr, _ = rpb.shape
    t = jnp.dot(rpb.astype(F32).reshape(h * nr, n_rel), jnp.asarray(onehot.reshape(n_rel, -1), F32),
                precision=lax.Precision.HIGHEST).reshape(h, nr, GRID_W, GRID_W)
    t = jnp.where(jnp.asarray(valid), t, NEG_BIAS)
    masked = jnp.full((h, 1, GRID_W, GRID_W), NEG_BIAS, F32)
    tpad = jnp.concatenate([masked, t, masked], axis=1)
    return jnp.concatenate([tpad[:, :-1], tpad[:, 1:]], axis=-1)


def _gateup_kernel(a_ref, b_ref, c_ref, wa_ref, wb_ref, wc_ref, ga_ref, gb_ref, gc_ref, o_ref, w_ref):
    @pl.when(pl.program_id(1) == 0)
    def _():
        for b, src in enumerate((wa_ref, wb_ref, wc_ref)):
            for r0 in range(0, BRANCH_W, CAST_ROWS):
                w_ref[b, r0:r0 + CAST_ROWS, :] = src[r0:r0 + CAST_ROWS, :].astype(BF16)

    acc = None
    for b, (br_ref, g_ref) in enumerate(((a_ref, ga_ref), (b_ref, gb_ref), (c_ref, gc_ref))):
        up = jnp.dot(br_ref[...], w_ref[b], preferred_element_type=F32)
        term = _sigmoid(g_ref[...].astype(F32)) * up
        acc = term if acc is None else acc + term
    o_ref[...] = acc.astype(o_ref.dtype)


def gate_up(br_a, br_b, br_c, w_up, p, layer, *, rows, row0s, p_row0):
    tm = _pick(rows, (1024, 512, 256))
    tn = 256

    def br_spec(row0):
        return pl.BlockSpec((tm, BRANCH_W), lambda j, i: (row0 // tm + i, 0))

    def w_spec(b):
        return pl.BlockSpec((None, None, BRANCH_W, tn), lambda j, i: (layer, b, 0, j))

    def g_spec(b):
        return pl.BlockSpec((tm, tn), lambda j, i: (p_row0 // tm + i, (OFF_GM + b * D_MODEL) // tn + j))

    return pl.pallas_call(
        _gateup_kernel,
        grid=(D_MODEL // tn, rows // tm),
        in_specs=[br_spec(row0s[0]), br_spec(row0s[1]), br_spec(row0s[2]),
                  w_spec(0), w_spec(1), w_spec(2), g_spec(0), g_spec(1), g_spec(2)],
        out_specs=pl.BlockSpec((tm, tn), lambda j, i: (i, j)),
        out_shape=jax.ShapeDtypeStruct((rows, D_MODEL), BF16),
        scratch_shapes=[pltpu.VMEM((N_BRANCH, BRANCH_W, tn), BF16)],
        compiler_params=_params(("arbitrary", "arbitrary"), 56),
        name="gate_up",
    )(br_a, br_b, br_c, w_up, w_up, w_up, p, p, p)


def _outproj_kernel(s_ref, w_ref, x_ref, mod_ref, o_ref, wb_ref, *, mod_row):
    @pl.when(pl.program_id(1) == 0)
    def _():
        for r0 in range(0, D_MODEL, CAST_ROWS):
            wb_ref[r0:r0 + CAST_ROWS, :] = w_ref[r0:r0 + CAST_ROWS, :].astype(BF16)

    y = jnp.dot(s_ref[...], wb_ref[...], preferred_element_type=F32)
    o_ref[...] = x_ref[...] + mod_ref[mod_row:mod_row + 1, :] * y


def out_proj(s, w_out, x, mod, layer, mod_row):
    rows = s.shape[0]
    tm = _pick(rows, (1024, 512, 256))
    tn = 512
    return pl.pallas_call(
        functools.partial(_outproj_kernel, mod_row=mod_row),
        grid=(D_MODEL // tn, rows // tm),
        in_specs=[pl.BlockSpec((tm, D_MODEL), lambda j, i: (i, 0)),
                  pl.BlockSpec((None, D_MODEL, tn), lambda j, i: (layer, 0, j)),
                  pl.BlockSpec((tm, tn), lambda j, i: (i, j)),
                  pl.BlockSpec((8, tn), lambda j, i: (0, 2 * D_MODEL // tn + j))],
        out_specs=pl.BlockSpec((tm, tn), lambda j, i: (i, j)),
        out_shape=jax.ShapeDtypeStruct((rows, D_MODEL), F32),
        scratch_shapes=[pltpu.VMEM((D_MODEL, tn), BF16)],
        compiler_params=_params(("arbitrary", "arbitrary"), 56),
        name="out_proj",
    )(s, w_out, x, mod)


def _final_norm_kernel(x_ref, w_ref, o_ref):
    xv = x_ref[...]
    o_ref[...] = xv * lax.rsqrt(jnp.mean(xv * xv, axis=-1, keepdims=True) + EPS) * w_ref[...]


def final_norm(x, w):
    t, d = x.shape
    tr = 256
    return pl.pallas_call(
        _final_norm_kernel,
        grid=(t // tr,),
        in_specs=[pl.BlockSpec((tr, d), lambda i: (i, 0)), pl.BlockSpec((1, d), lambda i: (0, 0))],
        out_specs=pl.BlockSpec((tr, d), lambda i: (i, 0)),
        out_shape=jax.ShapeDtypeStruct((t, d), F32),
        compiler_params=_params(("arbitrary",), 40),
        name="final_norm",
    )(x, w.reshape(1, d))


def _rope_tables(n_lat, n_ctx):
    pos = jnp.arange(n_lat, dtype=jnp.int32)
    row = (pos // GRID_W).astype(F32)
    colp = (pos % GRID_W).astype(F32)
    n_freq = HEAD_DIM // 4
    inv_freq = ROPE_THETA ** (-jnp.arange(n_freq, dtype=F32) / n_freq)
    ang = jnp.concatenate([row[:, None] * inv_freq, colp[:, None] * inv_freq], axis=-1)
    cos, sin = jnp.cos(ang), jnp.sin(ang)
    cos_t = jnp.repeat(cos, 2, axis=-1)
    sin_t = jnp.stack([-sin, sin], axis=-1).reshape(n_lat, HEAD_DIM)
    cos_t = jnp.concatenate([cos_t, jnp.ones((n_ctx, HEAD_DIM), F32)], axis=0)
    sin_t = jnp.concatenate([sin_t, jnp.zeros((n_ctx, HEAD_DIM), F32)], axis=0)
    return cos_t, sin_t


def hybrid_layer(x, ctx, sb, lp, cos_t, sin_t, update_ctx):
    t = x.shape[0]
    lc = ctx.shape[0]
    layer = lp["layer"]
    mod = modulation(sb, lp["w_mod"], lp["b_mod"], layer)
    h = norm_modulate(x, ctx, lp["norm_w"], mod)
    p = in_proj(h, lp["w_in"], lp["b_main"], layer)
    dt_raw = dt_proj(h, lp["w_in"], lp["b_dt"], layer)

    u = ssd_conv(p, lp["conv_w"], lp["conv_b"], t // SSM_CHUNK)
    br_a = ssd_branch(u, dt_raw, lp["ssd_cst"], p, lp["d_exp"], lp["ssm_norm_w"], lc // SSM_CHUNK)

    qn, kn = qk_prep(p, cos_t, sin_t, lp["q_norm_w"], lp["k_norm_w"])
    tq = _pick(t, (1024, 512, 256))
    tk = _pick(t + lc, (1408, 768, 640, 512, 384, 256))
    br_b = flash_gqa(qn, kn, p, q_rows=t, q_row0=0, k_rows=t + lc, k_row0=0, tq=tq, tk=tk)

    br_c = neighbourhood(p, lp["na_bias"], t)

    s = gate_up(br_a, br_b, br_c, lp["w_up"], p, layer, rows=t, row0s=(0, 0, 0), p_row0=0)
    x_new = out_proj(s, lp["w_out"], x, mod, layer, 0)
    if not update_ctx:
        return x_new, ctx
    br_b_c = flash_gqa(qn, kn, p, q_rows=lc, q_row0=t, k_rows=lc, k_row0=t, tq=lc, tk=lc)
    s_c = gate_up(br_a, br_b_c, br_c, lp["w_up"], p, layer, rows=lc, row0s=(t, 0, t), p_row0=t)
    ctx_new = out_proj(s_c, lp["w_out"], ctx, mod, layer, 1)
    return x_new, ctx_new


def layer_params(layer, w_mod, b_mod, norm_w, w_in, b_in, conv_w, conv_b, a_log, dt_bias, d_skip, ssm_norm_w,
                 q_norm_w, k_norm_w, rpb, w_up, w_out):
    pad_heads = [(0, 0), (0, LANES - DT_W)]
    b = b_in[layer].astype(F32)
    cst = jnp.concatenate([
        jnp.pad(dt_bias[layer].astype(F32).reshape(1, DT_W), pad_heads),
        jnp.pad(-jnp.exp(a_log[layer].astype(F32)).reshape(1, DT_W), pad_heads),
        jnp.zeros((6, LANES), F32)], axis=0)
    return dict(
        layer=layer, w_mod=w_mod, b_mod=b_mod[layer], norm_w=norm_w[layer], w_in=w_in,
        b_main=jnp.concatenate([b[:_SRC_DT], b[_SRC_QA:]]),
        b_dt=jnp.pad(b[_SRC_DT:_SRC_QA], (0, LANES - DT_W)),
        conv_w=conv_w[layer], conv_b=conv_b[layer], ssd_cst=cst,
        d_exp=jnp.repeat(d_skip[layer].astype(F32), SSM_HEAD_DIM).reshape(1, BRANCH_W),
        ssm_norm_w=ssm_norm_w[layer],
        q_norm_w=q_norm_w[layer].reshape(1, HEAD_DIM), k_norm_w=k_norm_w[layer].reshape(1, HEAD_DIM),
        na_bias=na_bias_table(rpb[layer]), w_up=w_up, w_out=w_out)


def kernel(x, c, ctx, c_ctx, norm_w, w_mod, b_mod, w_in, b_in, conv_w, conv_b, a_log, dt_bias, d_skip, ssm_norm_w, q_norm_w, k_norm_w, rpb, w_up, w_out, final_norm_w):
    xs = x[0]
    cs = ctx[0]
    sb = jnp.broadcast_to(jnp.stack([c[0], c_ctx])[:, :, None], (2, D_MODEL, LANES))
    cos_t, sin_t = _rope_tables(xs.shape[0], cs.shape[0])
    for layer in range(DEPTH):
        lp = layer_params(layer, w_mod, b_mod, norm_w, w_in, b_in, conv_w, conv_b, a_log, dt_bias, d_skip,
                          ssm_norm_w, q_norm_w, k_norm_w, rpb, w_up, w_out)
        xs, cs = hybrid_layer(xs, cs, sb, lp, cos_t, sin_t, layer < DEPTH - 1)
    return final_norm(xs, final_norm_w)[None]
```

```python
import functools
import math

import numpy as np
import jax
import jax.numpy as jnp
from jax import lax
from jax.experimental import pallas as pl
from jax.experimental.pallas import tpu as pltpu

F32 = jnp.float32
BF16 = jnp.bfloat16

D_MODEL = 4096
DEPTH = 2
GRID_W = 64
N_BRANCH = 3
BRANCH_W = 2048
EPS = 1e-6

SSM_HEADS = 32
SSM_HEAD_DIM = 64
SSM_GROUPS = 4
SSM_STATE = 128
SSM_CHUNK = 128
CONV_K = 5
SSM_GN = SSM_GROUPS * SSM_STATE
XBC_W = BRANCH_W + 2 * SSM_GN
DT_W = 2 * SSM_HEADS
GROUP_W = (SSM_HEADS // SSM_GROUPS) * SSM_HEAD_DIM

ATT_HEADS = 16
KV_HEADS = 4
ATT_GROUP = ATT_HEADS // KV_HEADS
HEAD_DIM = 128
ROPE_THETA = 10000.0

NA_HEADS = 16
NA_WIN_R = 8
NA_WIN_C = 16

LANES = 128
MIB = 1024 * 1024

_SRC_DT = XBC_W + BRANCH_W
_SRC_QA = _SRC_DT + 2 * SSM_HEADS
_SRC_VA = _SRC_QA + BRANCH_W + KV_HEADS * HEAD_DIM
OFF_XBC = 0
OFF_Z = OFF_XBC + XBC_W
OFF_QA = OFF_Z + BRANCH_W
OFF_KA = OFF_QA + BRANCH_W
OFF_VA = OFF_KA + KV_HEADS * HEAD_DIM
OFF_GA = OFF_VA + KV_HEADS * HEAD_DIM
OFF_QN = OFF_GA + BRANCH_W
OFF_KN = OFF_QN + BRANCH_W
OFF_VN = OFF_KN + BRANCH_W
OFF_GN = OFF_VN + BRANCH_W
OFF_GM = OFF_GN + BRANCH_W
P_COLS = OFF_GM + N_BRANCH * D_MODEL
QK_W = BRANCH_W + KV_HEADS * HEAD_DIM

NEG_BIAS = -1e30


def _params(sem, vmem_mib):
    return pltpu.CompilerParams(dimension_semantics=sem, vmem_limit_bytes=vmem_mib * MIB)


def _pick(n, candidates):
    for c in candidates:
        if n % c == 0:
            return c
    raise ValueError(f"no tile for {n} in {candidates}")


def _sigmoid(v):
    return 1.0 / (1.0 + jnp.exp(-v))


def _silu(v):
    return v * _sigmoid(v)


def _mod_kernel(s_ref, w_ref, b_ref, o_ref):
    tn = w_ref.shape[1]
    k = w_ref.shape[0]
    o_ref[...] = jnp.zeros(o_ref.shape, F32)
    for v in range(2):
        s = _silu(s_ref[v])
        for jb in range(tn // LANES):
            sl = slice(jb * LANES, (jb + 1) * LANES)
            prod = w_ref[:, sl] * s
            part = prod.reshape(k // 8, 8, LANES).sum(axis=0)
            o_ref[v:v + 1, sl] = part.sum(axis=0, keepdims=True) + b_ref[:, sl]


def modulation(sb, w_mod, b_mod, layer):
    _, k, n = w_mod.shape
    tn = 512
    return pl.pallas_call(
        _mod_kernel,
        grid=(n // tn,),
        in_specs=[pl.BlockSpec((2, k, LANES), lambda j: (0, 0, 0)),
                  pl.BlockSpec((None, k, tn), lambda j: (layer, 0, j)),
                  pl.BlockSpec((1, tn), lambda j: (0, j))],
        out_specs=pl.BlockSpec((8, tn), lambda j: (0, j)),
        out_shape=jax.ShapeDtypeStruct((8, n), F32),
        compiler_params=_params(("arbitrary",), 40),
        name="modulation",
    )(sb, w_mod, b_mod.reshape(1, n))


def _norm_kernel(x_ref, c_ref, nw_ref, mod_ref, o_ref, *, n_lat_blocks):
    i = pl.program_id(0)

    def emit(src_ref, row):
        xv = src_ref[...]
        y = xv * lax.rsqrt(jnp.mean(xv * xv, axis=-1, keepdims=True) + EPS) * nw_ref[...]
        shift = mod_ref[row:row + 1, 0:D_MODEL]
        scale = mod_ref[row:row + 1, D_MODEL:2 * D_MODEL]
        o_ref[...] = (y * (1.0 + scale) + shift).astype(o_ref.dtype)

    @pl.when(i < n_lat_blocks)
    def _():
        emit(x_ref, 0)

    @pl.when(i >= n_lat_blocks)
    def _():
        emit(c_ref, 1)


def norm_modulate(x, ctx, norm_w, mod):
    t, d = x.shape
    lc = ctx.shape[0]
    tr = 256
    n_lat = t // tr
    n_ctx = lc // tr
    return pl.pallas_call(
        functools.partial(_norm_kernel, n_lat_blocks=n_lat),
        grid=(n_lat + n_ctx,),
        in_specs=[pl.BlockSpec((tr, d), lambda i: (jnp.minimum(i, n_lat - 1), 0)),
                  pl.BlockSpec((tr, d), lambda i: (jnp.maximum(i - n_lat, 0), 0)),
                  pl.BlockSpec((1, d), lambda i: (0, 0)),
                  pl.BlockSpec((8, 3 * d), lambda i: (0, 0))],
        out_specs=pl.BlockSpec((tr, d), lambda i: (i, 0)),
        out_shape=jax.ShapeDtypeStruct((t + lc, d), BF16),
        compiler_params=_params(("arbitrary",), 40),
        name="norm_modulate",
    )(x, ctx, norm_w.reshape(1, d), mod)


NT_DIMS = (((1,), (1,)), ((), ()))


def _wcast_kernel(wa_ref, wb_ref, o_ref, *, n_plain):
    j = pl.program_id(0)
    tn = wa_ref.shape[0]

    @pl.when(j < n_plain)
    def _():
        o_ref[...] = wa_ref[...].astype(BF16)

    @pl.when(j >= n_plain)
    def _():
        o_ref[0:tn - DT_W, :] = wa_ref[DT_W:tn, :].astype(BF16)
        o_ref[tn - DT_W:tn, :] = wb_ref[0:DT_W, :].astype(BF16)


def in_proj_weight(w_t, layer):
    k = w_t.shape[2]
    tn = 512
    sub = tn // LANES
    return pl.pallas_call(
        functools.partial(_wcast_kernel, n_plain=_SRC_DT // tn),
        grid=(P_COLS // tn,),
        in_specs=[pl.BlockSpec((None, tn, k), lambda j: (layer, j, 0)),
                  pl.BlockSpec((None, LANES, k), lambda j: (layer, (j + 1) * sub, 0))],
        out_specs=pl.BlockSpec((tn, k), lambda j: (j, 0)),
        out_shape=jax.ShapeDtypeStruct((P_COLS, k), BF16),
        compiler_params=_params(("arbitrary",), 40),
        name="in_proj_weight",
    )(w_t, w_t)


def _inproj_kernel(h_ref, w_ref, b_ref, o_ref):
    acc = lax.dot_general(h_ref[...], w_ref[...], NT_DIMS, preferred_element_type=F32)
    o_ref[...] = (acc + b_ref[...]).astype(o_ref.dtype)


def in_proj(h, w_b, b_main):
    m, k = h.shape
    tm = _pick(m, (1408, 1280, 1024, 768, 640, 512, 256, 128))
    tn = 512
    return pl.pallas_call(
        _inproj_kernel,
        grid=(m // tm, P_COLS // tn),
        in_specs=[pl.BlockSpec((tm, k), lambda i, j: (i, 0)),
                  pl.BlockSpec((tn, k), lambda i, j: (j, 0)),
                  pl.BlockSpec((1, tn), lambda i, j: (0, j))],
        out_specs=pl.BlockSpec((tm, tn), lambda i, j: (i, j)),
        out_shape=jax.ShapeDtypeStruct((m, P_COLS), BF16),
        compiler_params=_params(("arbitrary", "arbitrary"), 48),
        name="in_proj",
    )(h, w_b, b_main.reshape(1, P_COLS))


def _dtproj_kernel(h_ref, w_ref, b_ref, o_ref):
    acc = lax.dot_general(h_ref[...], w_ref[...].astype(BF16), NT_DIMS, preferred_element_type=F32)
    o_ref[...] = acc + b_ref[...]


def dt_proj(h, w_t, b_dt, layer):
    m, k = h.shape
    tm = _pick(m, (1408, 1280, 1024, 768, 640, 512, 256, 128))
    return pl.pallas_call(
        _dtproj_kernel,
        grid=(m // tm,),
        in_specs=[pl.BlockSpec((tm, k), lambda i: (i, 0)),
                  pl.BlockSpec((None, LANES, k), lambda i: (layer, _SRC_DT // LANES, 0)),
                  pl.BlockSpec((1, LANES), lambda i: (0, 0))],
        out_specs=pl.BlockSpec((tm, LANES), lambda i: (i, 0)),
        out_shape=jax.ShapeDtypeStruct((m, LANES), F32),
        compiler_params=_params(("arbitrary",), 48),
        name="dt_proj",
    )(h, w_t, b_dt.reshape(1, LANES))


def _conv_kernel(prev_ref, cur_ref, next_ref, w_ref, b_ref, o_ref, *, n_lat_chunks, n_chunks):
    c = pl.program_id(0)
    has_prev = jnp.logical_and(c != 0, c != n_lat_chunks)
    has_next = jnp.logical_and(c != n_lat_chunks - 1, c != n_chunks - 1)
    pad = CONV_K // 2
    cw = 512
    for jb in range(XBC_W // cw):
        sl = slice(jb * cw, (jb + 1) * cw)
        cur = cur_ref[:, sl].astype(F32)
        pv = prev_ref[SSM_CHUNK - 16:SSM_CHUNK, sl].astype(F32)[8:16]
        nx = next_ref[0:16, sl].astype(F32)[0:8]
        pv = jnp.where(has_prev, pv, 0.0)
        nx = jnp.where(has_next, nx, 0.0)
        ext = jnp.concatenate([pv, cur, nx], axis=0)
        acc = jnp.broadcast_to(b_ref[:, sl], (SSM_CHUNK, cw))
        for kk in range(CONV_K):
            lo = 8 + kk - pad
            acc = acc + w_ref[kk:kk + 1, sl] * ext[lo:lo + SSM_CHUNK]
        o_ref[:, sl] = _silu(acc).astype(o_ref.dtype)


def ssd_conv(p, conv_w, conv_b, n_lat_chunks):
    r = p.shape[0]
    n_chunks = r // SSM_CHUNK

    def prev_map(c):
        bad = jnp.logical_or(c == 0, c == n_lat_chunks)
        return (jnp.where(bad, c, c - 1), 0)

    def next_map(c):
        bad = jnp.logical_or(c == n_lat_chunks - 1, c == n_chunks - 1)
        return (jnp.where(bad, c, c + 1), 0)

    blk = (SSM_CHUNK, XBC_W)
    return pl.pallas_call(
        functools.partial(_conv_kernel, n_lat_chunks=n_lat_chunks, n_chunks=n_chunks),
        grid=(n_chunks,),
        in_specs=[pl.BlockSpec(blk, prev_map),
                  pl.BlockSpec(blk, lambda c: (c, 0)),
                  pl.BlockSpec(blk, next_map),
                  pl.BlockSpec((8, XBC_W), lambda c: (0, 0)),
                  pl.BlockSpec((1, XBC_W), lambda c: (0, 0))],
        out_specs=pl.BlockSpec(blk, lambda c: (c, 0)),
        out_shape=jax.ShapeDtypeStruct((r, XBC_W), BF16),
        compiler_params=_params(("arbitrary",), 32),
        name="ssd_conv",
    )(p, p, p, jnp.pad(conv_w, ((0, 8 - CONV_K), (0, 0))), conv_b.reshape(1, XBC_W))


def _dot3(tri, v):
    hi = v.astype(BF16)
    r1 = v - hi.astype(F32)
    mid = r1.astype(BF16)
    lo = (r1 - mid.astype(F32)).astype(BF16)
    return (jnp.dot(tri, hi, preferred_element_type=F32)
            + jnp.dot(tri, mid, preferred_element_type=F32)
            + jnp.dot(tri, lo, preferred_element_type=F32))


def _pair_cols(v, ha, lane_lo):
    rows = v.shape[0]
    a = jnp.broadcast_to(v[:, ha:ha + 1], (rows, LANES))
    b = jnp.broadcast_to(v[:, ha + 1:ha + 2], (rows, LANES))
    return jnp.where(lane_lo, a, b)


def _ssd_chunk(u_ref, dt_ref, cst_ref, ht_ref, emit, *, reverse):
    L = SSM_CHUNK
    hoff = SSM_HEADS if reverse else 0
    li = lax.broadcasted_iota(jnp.int32, (L, L), 0)
    si = lax.broadcasted_iota(jnp.int32, (L, L), 1)
    keep = (li <= si) if reverse else (li >= si)
    tri = jnp.where((si >= li) if reverse else (si <= li), 1.0, 0.0).astype(BF16)
    lane_lo = lax.broadcasted_iota(jnp.int32, (L, LANES), 1) < SSM_HEAD_DIM
    lane_lo1 = lax.broadcasted_iota(jnp.int32, (1, LANES), 1) < SSM_HEAD_DIM

    raw = dt_ref[...] + cst_ref[0:1, :]
    dtv = jnp.maximum(raw, 0.0) + jnp.log1p(jnp.exp(-jnp.abs(raw)))
    da = dtv * cst_ref[1:2, :]
    acs = _dot3(tri, da)
    acs_t = acs.T
    dt_t = dtv.T
    a_last = acs[0:1, :] if reverse else acs[L - 1:L, :]
    wst = jnp.exp(a_last - acs) * dtv
    eacs = jnp.exp(acs)
    e_last = jnp.exp(a_last)

    pairs = GROUP_W // LANES
    for g in range(SSM_GROUPS):
        bg = u_ref[:, BRANCH_W + g * SSM_STATE:BRANCH_W + (g + 1) * SSM_STATE]
        cg = u_ref[:, BRANCH_W + SSM_GN + g * SSM_STATE:BRANCH_W + SSM_GN + (g + 1) * SSM_STATE]
        cb = lax.dot_general(cg, bg, (((1,), (1,)), ((), ())), preferred_element_type=F32)
        bg_t = bg.astype(F32).T.astype(BF16)
        y_parts, xw_parts, e_parts, d_parts = [], [], [], []
        for pi in range(pairs):
            ha = hoff + g * (2 * pairs) + 2 * pi
            xp = u_ref[:, g * GROUP_W + pi * LANES:g * GROUP_W + (pi + 1) * LANES]
            zero = jnp.zeros_like(xp)
            y_pair = None
            for hh, xm in ((ha, jnp.where(lane_lo, xp, zero)), (ha + 1, jnp.where(lane_lo, zero, xp))):
                diff = acs[:, hh:hh + 1] - acs_t[hh:hh + 1, :]
                dec = jnp.exp(jnp.where(keep, diff, -jnp.inf))
                m = (cb * dec * dt_t[hh:hh + 1, :]).astype(BF16)
                yh = jnp.dot(m, xm, preferred_element_type=F32)
                y_pair = yh if y_pair is None else y_pair + yh
            y_parts.append(y_pair)
            xw_parts.append((xp.astype(F32) * _pair_cols(wst, ha, lane_lo)).astype(BF16))
            e_parts.append(_pair_cols(eacs, ha, lane_lo))
            d_parts.append(_pair_cols(e_last, ha, lane_lo1))
        hs = ht_ref[g]
        y_off = jnp.dot(cg, hs.astype(BF16), preferred_element_type=F32) * jnp.concatenate(e_parts, axis=1)
        s_new = jnp.dot(bg_t, jnp.concatenate(xw_parts, axis=1), preferred_element_type=F32)
        ht_ref[g] = hs * jnp.concatenate(d_parts, axis=1) + s_new
        emit(g, jnp.concatenate(y_parts, axis=1) + y_off)


def _ssd_fwd_kernel(u_ref, dt_ref, cst_ref, o_ref, ht_ref):
    @pl.when(pl.program_id(0) == 0)
    def _():
        ht_ref[...] = jnp.zeros(ht_ref.shape, F32)

    def emit(g, y):
        o_ref[:, g * GROUP_W:(g + 1) * GROUP_W] = y

    _ssd_chunk(u_ref, dt_ref, cst_ref, ht_ref, emit, reverse=False)


def _ssd_bwd_kernel(u_ref, dt_ref, cst_ref, yf_ref, z0_ref, z1_ref, dsk_ref, nw_ref, o_ref, ht_ref, y_ref):
    @pl.when(pl.program_id(0) == 0)
    def _():
        ht_ref[...] = jnp.zeros(ht_ref.shape, F32)

    half = SSM_GROUPS // 2

    def emit(g, y):
        sl = slice(g * GROUP_W, (g + 1) * GROUP_W)
        z_ref = z0_ref if g < half else z1_ref
        zsl = slice((g % half) * GROUP_W, (g % half + 1) * GROUP_W)
        tot = y + yf_ref[:, sl] + dsk_ref[:, sl] * u_ref[:, sl].astype(F32)
        y_ref[:, sl] = tot * _silu(z_ref[:, zsl].astype(F32))

    _ssd_chunk(u_ref, dt_ref, cst_ref, ht_ref, emit, reverse=True)
    yv = y_ref[...]
    o_ref[...] = (yv * lax.rsqrt(jnp.mean(yv * yv, axis=-1, keepdims=True) + EPS) * nw_ref[...]).astype(o_ref.dtype)


def ssd_branch(u, dt_raw, cst, p, d_exp, ssm_norm_w, n_ctx_chunks):
    r = u.shape[0]
    n_chunks = r // SSM_CHUNK
    n_lat = n_chunks - n_ctx_chunks
    state = pltpu.VMEM((SSM_GROUPS, SSM_STATE, GROUP_W), F32)

    def fwd_idx(i):
        return jnp.where(i < n_ctx_chunks, n_lat + i, i - n_ctx_chunks)

    def bwd_idx(i):
        return n_chunks - 1 - i

    ublk = (SSM_CHUNK, XBC_W)
    yf = pl.pallas_call(
        _ssd_fwd_kernel,
        grid=(n_chunks,),
        in_specs=[pl.BlockSpec(ublk, lambda i: (fwd_idx(i), 0)),
                  pl.BlockSpec((SSM_CHUNK, LANES), lambda i: (fwd_idx(i), 0)),
                  pl.BlockSpec((8, LANES), lambda i: (0, 0))],
        out_specs=pl.BlockSpec((SSM_CHUNK, BRANCH_W), lambda i: (fwd_idx(i), 0)),
        out_shape=jax.ShapeDtypeStruct((r, BRANCH_W), F32),
        scratch_shapes=[state],
        compiler_params=_params(("arbitrary",), 32),
        name="ssd_fwd",
    )(u, dt_raw, cst)

    zw = BRANCH_W // 2
    z0 = OFF_Z // zw
    return pl.pallas_call(
        _ssd_bwd_kernel,
        grid=(n_chunks,),
        in_specs=[pl.BlockSpec(ublk, lambda i: (bwd_idx(i), 0)),
                  pl.BlockSpec((SSM_CHUNK, LANES), lambda i: (bwd_idx(i), 0)),
                  pl.BlockSpec((8, LANES), lambda i: (0, 0)),
                  pl.BlockSpec((SSM_CHUNK, BRANCH_W), lambda i: (bwd_idx(i), 0)),
                  pl.BlockSpec((SSM_CHUNK, zw), lambda i: (bwd_idx(i), z0)),
                  pl.BlockSpec((SSM_CHUNK, zw), lambda i: (bwd_idx(i), z0 + 1)),
                  pl.BlockSpec((1, BRANCH_W), lambda i: (0, 0)),
                  pl.BlockSpec((1, BRANCH_W), lambda i: (0, 0))],
        out_specs=pl.BlockSpec((SSM_CHUNK, BRANCH_W), lambda i: (bwd_idx(i), 0)),
        out_shape=jax.ShapeDtypeStruct((r, BRANCH_W), BF16),
        scratch_shapes=[state, pltpu.VMEM((SSM_CHUNK, BRANCH_W), F32)],
        compiler_params=_params(("arbitrary",), 32),
        name="ssd_bwd",
    )(u, dt_raw, cst, yf, p, p, d_exp, ssm_norm_w.reshape(1, BRANCH_W))


def _qkprep_kernel(qk_ref, cos_ref, sin_ref, qw_ref, kw_ref, q_ref, k_ref):
    cosv = cos_ref[...]
    sinv = sin_ref[...]
    scale = HEAD_DIM ** -0.5 * math.log2(math.e)
    even = lax.broadcasted_iota(jnp.int32, cosv.shape, 1) % 2 == 0
    for h in range(ATT_HEADS + KV_HEADS):
        t = qk_ref[:, h * HEAD_DIM:(h + 1) * HEAD_DIM].astype(F32)
        w = qw_ref[...] if h < ATT_HEADS else kw_ref[...]
        n = t * lax.rsqrt(jnp.mean(t * t, axis=-1, keepdims=True) + EPS) * w
        partner = jnp.where(even, pltpu.roll(n, HEAD_DIM - 1, 1), pltpu.roll(n, 1, 1))
        rot = n * cosv + partner * sinv
        if h < ATT_HEADS:
            q_ref[:, h * HEAD_DIM:(h + 1) * HEAD_DIM] = (rot * scale).astype(q_ref.dtype)
        else:
            hk = h - ATT_HEADS
            k_ref[:, hk * HEAD_DIM:(hk + 1) * HEAD_DIM] = rot.astype(k_ref.dtype)


def qk_prep(p, cos_t, sin_t, qw, kw):
    r = p.shape[0]
    tr = 256
    kvw = KV_HEADS * HEAD_DIM
    return pl.pallas_call(
        _qkprep_kernel,
        grid=(r // tr,),
        in_specs=[pl.BlockSpec((tr, QK_W), lambda i: (i, OFF_QA // QK_W)),
                  pl.BlockSpec((tr, HEAD_DIM), lambda i: (i, 0)),
                  pl.BlockSpec((tr, HEAD_DIM), lambda i: (i, 0)),
                  pl.BlockSpec((1, HEAD_DIM), lambda i: (0, 0)),
                  pl.BlockSpec((1, HEAD_DIM), lambda i: (0, 0))],
        out_specs=[pl.BlockSpec((tr, BRANCH_W), lambda i: (i, 0)),
                   pl.BlockSpec((tr, kvw), lambda i: (i, 0))],
        out_shape=[jax.ShapeDtypeStruct((r, BRANCH_W), BF16),
                   jax.ShapeDtypeStruct((r, kvw), BF16)],
        compiler_params=_params(("arbitrary",), 32),
        name="qk_prep",
    )(p, cos_t, sin_t, qw, kw)


def _flash_kernel(q_ref, k_ref, v_ref, g_ref, o_ref, m_ref, l_ref, acc_ref, *, nk):
    ki = pl.program_id(2)

    @pl.when(ki == 0)
    def _():
        m_ref[...] = jnp.full(m_ref.shape, -jnp.inf, F32)
        l_ref[...] = jnp.zeros(l_ref.shape, F32)
        acc_ref[...] = jnp.zeros(acc_ref.shape, F32)

    k = k_ref[...]
    v1 = jnp.concatenate([v_ref[...], jnp.ones(v_ref.shape, v_ref.dtype)], axis=1)
    reps = k.shape[0] // LANES
    for r in range(ATT_GROUP):
        hs = slice(r * HEAD_DIM, (r + 1) * HEAD_DIM)
        s = lax.dot_general(q_ref[:, hs], k, (((1,), (1,)), ((), ())), preferred_element_type=F32)
        m_prev = m_ref[r]
        m_next = jnp.maximum(m_prev, jnp.max(s, axis=1, keepdims=True))
        p = jnp.exp2(s - jnp.concatenate([m_next] * reps, axis=1))
        alpha = jnp.exp2(m_prev - m_next)
        pv = jnp.dot(p.astype(v1.dtype), v1, preferred_element_type=F32)
        l_ref[r] = alpha * l_ref[r] + pv[:, HEAD_DIM:]
        acc_ref[r] = alpha * acc_ref[r] + pv[:, :HEAD_DIM]
        m_ref[r] = m_next

    @pl.when(ki == nk - 1)
    def _():
        for r in range(ATT_GROUP):
            hs = slice(r * HEAD_DIM, (r + 1) * HEAD_DIM)
            o = acc_ref[r] / l_ref[r]
            o_ref[:, hs] = (_silu(g_ref[:, hs].astype(F32)) * o).astype(o_ref.dtype)


def flash_gqa(qn, kn, p, *, q_rows, q_row0, k_rows, k_row0, tq, tk):
    nq = q_rows // tq
    nk = k_rows // tk
    qb0 = q_row0 // tq
    kb0 = k_row0 // tk
    gw = ATT_GROUP * HEAD_DIM
    return pl.pallas_call(
        functools.partial(_flash_kernel, nk=nk),
        grid=(KV_HEADS, nq, nk),
        in_specs=[pl.BlockSpec((tq, gw), lambda g, i, j: (qb0 + i, g)),
                  pl.BlockSpec((tk, HEAD_DIM), lambda g, i, j: (kb0 + j, g)),
                  pl.BlockSpec((tk, HEAD_DIM), lambda g, i, j: (kb0 + j, OFF_VA // HEAD_DIM + g)),
                  pl.BlockSpec((tq, gw), lambda g, i, j: (qb0 + i, OFF_GA // gw + g))],
        out_specs=pl.BlockSpec((tq, gw), lambda g, i, j: (i, g)),
        out_shape=jax.ShapeDtypeStruct((q_rows, BRANCH_W), BF16),
        scratch_shapes=[pltpu.VMEM((ATT_GROUP, tq, LANES), F32),
                        pltpu.VMEM((ATT_GROUP, tq, LANES), F32),
                        pltpu.VMEM((ATT_GROUP, tq, HEAD_DIM), F32)],
        compiler_params=_params(("arbitrary", "arbitrary", "arbitrary"), 48),
        name="flash_gqa",
    )(qn, kn, p, p)


NA_QROWS = 4
NA_KROWS = 12
NA_PATTERNS = 3


def _na_rel_row(pattern, qr, kr):
    if pattern == 0:
        return kr - qr + (NA_WIN_R - 1) if kr < NA_WIN_R else None
    if pattern == 1:
        return kr - qr + (NA_WIN_R - 1 - NA_WIN_R // 2) if 0 <= kr - qr < NA_WIN_R else None
    first = NA_KROWS - NA_WIN_R
    return kr - qr + (NA_WIN_R - 1) - (NA_KROWS - NA_QROWS) if kr >= first else None


def _na_kernel(q_ref, k_ref, v_ref, g_ref, tp_ref, o_ref, bias_ref, *, n_lat, n_ctx, grid_rows):
    scale = HEAD_DIM ** -0.5
    qn = NA_QROWS * GRID_W
    win = NA_KROWS * GRID_W
    n_blocks = grid_rows // NA_QROWS
    kc = k_ref[n_lat:n_lat + n_ctx, :]
    vc = v_ref[n_lat:n_lat + n_ctx, :]
    nt = (((1,), (1,)), ((), ()))

    lane_lo = lax.broadcasted_iota(jnp.int32, (GRID_W, LANES), 1) < GRID_W
    for pattern in range(NA_PATTERNS):
        for qr in range(NA_QROWS):
            for kp in range(NA_KROWS // 2):
                ja = _na_rel_row(pattern, qr, 2 * kp)
                jb = _na_rel_row(pattern, qr, 2 * kp + 1)
                if ja is None and jb is None:
                    tile = jnp.full((GRID_W, LANES), NEG_BIAS, F32)
                elif jb is None:
                    tile = jnp.where(lane_lo, tp_ref[ja + 1], NEG_BIAS)
                elif ja is None:
                    tile = jnp.where(lane_lo, NEG_BIAS, tp_ref[jb])
                else:
                    tile = tp_ref[jb]
                bias_ref[pattern, qr * GRID_W:(qr + 1) * GRID_W, kp * LANES:(kp + 1) * LANES] = tile

    def block_body(b, carry):
        u0 = jnp.clip(b * NA_QROWS - NA_WIN_R // 2, 0, grid_rows - NA_KROWS)
        pattern = jnp.where(b == 0, 0, jnp.where(b == n_blocks - 1, 2, 1))
        qs = pl.ds(pl.multiple_of(b * qn, qn), qn)
        ws = pl.ds(pl.multiple_of(u0 * GRID_W, GRID_W), win)
        q = q_ref[qs, :]
        s_loc = lax.dot_general(q, k_ref[ws, :], nt, preferred_element_type=F32) * scale + bias_ref[pattern]
        s_ctx = lax.dot_general(q, kc, nt, preferred_element_type=F32) * scale
        m = jnp.maximum(jnp.max(s_loc, axis=1, keepdims=True), jnp.max(s_ctx, axis=1, keepdims=True))
        p_loc = jnp.exp(s_loc - m)
        p_ctx = jnp.exp(s_ctx - m)
        den = jnp.sum(p_loc, axis=1, keepdims=True) + jnp.sum(p_ctx, axis=1, keepdims=True)
        o = (jnp.dot(p_loc.astype(BF16), v_ref[ws, :], preferred_element_type=F32)
             + jnp.dot(p_ctx.astype(BF16), vc, preferred_element_type=F32)) / den
        o_ref[qs, :] = (_silu(g_ref[qs, :].astype(F32)) * o).astype(o_ref.dtype)
        return carry

    lax.fori_loop(0, n_blocks, block_body, 0, unroll=2)

    qs = slice(n_lat, n_lat + n_ctx)
    s = lax.dot_general(q_ref[qs, :], kc, nt, preferred_element_type=F32) * scale
    pc = jnp.exp(s - jnp.max(s, axis=1, keepdims=True))
    oc = jnp.dot(pc.astype(BF16), vc, preferred_element_type=F32) / jnp.sum(pc, axis=1, keepdims=True)
    o_ref[qs, :] = (_silu(g_ref[qs, :].astype(F32)) * oc).astype(o_ref.dtype)


def neighbourhood(p, tp, n_lat):
    r = p.shape[0]
    n_ctx = r - n_lat
    col = lambda off: (lambda h: (0, off // HEAD_DIM + h))
    return pl.pallas_call(
        functools.partial(_na_kernel, n_lat=n_lat, n_ctx=n_ctx, grid_rows=n_lat // GRID_W),
        grid=(NA_HEADS,),
        in_specs=[pl.BlockSpec((r, HEAD_DIM), col(OFF_QN)),
                  pl.BlockSpec((r, HEAD_DIM), col(OFF_KN)),
                  pl.BlockSpec((r, HEAD_DIM), col(OFF_VN)),
                  pl.BlockSpec((r, HEAD_DIM), col(OFF_GN)),
                  pl.BlockSpec((None, 2 * NA_WIN_R, GRID_W, 2 * GRID_W), lambda h: (h, 0, 0, 0))],
        out_specs=pl.BlockSpec((r, HEAD_DIM), lambda h: (0, h)),
        out_shape=jax.ShapeDtypeStruct((r, BRANCH_W), BF16),
        scratch_shapes=[pltpu.VMEM((NA_PATTERNS, NA_QROWS * GRID_W, NA_KROWS * GRID_W), F32)],
        compiler_params=_params(("arbitrary",), 48),
        name="neighbourhood",
    )(p, p, p, p, tp)


def na_bias_table(rpb):
    n_rel = 2 * NA_WIN_C - 1
    c = np.arange(GRID_W)[:, None]
    kc = np.arange(GRID_W)[None, :]
    c0 = np.clip(c - NA_WIN_C // 2, 0, GRID_W - NA_WIN_C)
    valid = (kc >= c0) & (kc < c0 + NA_WIN_C)
    rel = kc - c + (NA_WIN_C - 1)
    onehot = (rel[None] == np.arange(n_rel)[:, None, None]) & valid[None]
    h, nr, _ = rpb.shape
    t = jnp.dot(rpb.astype(F32).reshape(h * nr, n_rel), jnp.asarray(onehot.reshape(n_rel, -1), F32),
                precision=lax.Precision.HIGHEST).reshape(h, nr, GRID_W, GRID_W)
    t = jnp.where(jnp.asarray(valid), t, NEG_BIAS)
    masked = jnp.full((h, 1, GRID_W, GRID_W), NEG_BIAS, F32)
    tpad = jnp.concatenate([masked, t, masked], axis=1)
    return jnp.concatenate([tpad[:, :-1], tpad[:, 1:]], axis=-1)


def _gateup_kernel(a_ref, b_ref, c_ref, wa_ref, wb_ref, wc_ref, ga_ref, gb_ref, gc_ref, o_ref):
    acc = None
    for br_ref, w_ref, g_ref in ((a_ref, wa_ref, ga_ref), (b_ref, wb_ref, gb_ref), (c_ref, wc_ref, gc_ref)):
        up = jnp.dot(br_ref[...], w_ref[...], preferred_element_type=F32)
        term = _sigmoid(g_ref[...].astype(F32)) * up
        acc = term if acc is None else acc + term
    o_ref[...] = acc.astype(o_ref.dtype)


def gate_up(br_a, br_b, br_c, w_up, p, layer, *, rows, row0s, p_row0):
    tm = _pick(rows, (1024, 512, 256))
    tn = 512

    def br_spec(row0):
        return pl.BlockSpec((tm, BRANCH_W), lambda i, j: (row0 // tm + i, 0))

    def w_spec(b):
        return pl.BlockSpec((None, None, BRANCH_W, tn), lambda i, j: (layer, b, 0, j))

    def g_spec(b):
        return pl.BlockSpec((tm, tn), lambda i, j: (p_row0 // tm + i, (OFF_GM + b * D_MODEL) // tn + j))

    return pl.pallas_call(
        _gateup_kernel,
        grid=(rows // tm, D_MODEL // tn),
        in_specs=[br_spec(row0s[0]), br_spec(row0s[1]), br_spec(row0s[2]),
                  w_spec(0), w_spec(1), w_spec(2), g_spec(0), g_spec(1), g_spec(2)],
        out_specs=pl.BlockSpec((tm, tn), lambda i, j: (i, j)),
        out_shape=jax.ShapeDtypeStruct((rows, D_MODEL), BF16),
        compiler_params=_params(("arbitrary", "arbitrary"), 56),
        name="gate_up",
    )(br_a, br_b, br_c, w_up, w_up, w_up, p, p, p)


def _outproj_kernel(s_ref, w_ref, x_ref, mod_ref, o_ref, *, mod_row):
    y = jnp.dot(s_ref[...], w_ref[...], preferred_element_type=F32)
    o_ref[...] = x_ref[...] + mod_ref[mod_row:mod_row + 1, :] * y


def out_proj(s, w_out, x, mod, layer, mod_row):
    rows = s.shape[0]
    tm = _pick(rows, (1024, 512, 256))
    tn = 512
    return pl.pallas_call(
        functools.partial(_outproj_kernel, mod_row=mod_row),
        grid=(rows // tm, D_MODEL // tn),
        in_specs=[pl.BlockSpec((tm, D_MODEL), lambda i, j: (i, 0)),
                  pl.BlockSpec((None, D_MODEL, tn), lambda i, j: (layer, 0, j)),
                  pl.BlockSpec((tm, tn), lambda i, j: (i, j)),
                  pl.BlockSpec((8, tn), lambda i, j: (0, 2 * D_MODEL // tn + j))],
        out_specs=pl.BlockSpec((tm, tn), lambda i, j: (i, j)),
        out_shape=jax.ShapeDtypeStruct((rows, D_MODEL), F32),
        compiler_params=_params(("arbitrary", "arbitrary"), 48),
        name="out_proj",
    )(s, w_out, x, mod)


def _final_norm_kernel(x_ref, w_ref, o_ref):
    xv = x_ref[...]
    o_ref[...] = xv * lax.rsqrt(jnp.mean(xv * xv, axis=-1, keepdims=True) + EPS) * w_ref[...]


def final_norm(x, w):
    t, d = x.shape
    tr = 256
    return pl.pallas_call(
        _final_norm_kernel,
        grid=(t // tr,),
        in_specs=[pl.BlockSpec((tr, d), lambda i: (i, 0)), pl.BlockSpec((1, d), lambda i: (0, 0))],
        out_specs=pl.BlockSpec((tr, d), lambda i: (i, 0)),
        out_shape=jax.ShapeDtypeStruct((t, d), F32),
        compiler_params=_params(("arbitrary",), 40),
        name="final_norm",
    )(x, w.reshape(1, d))


def _rope_tables(n_lat, n_ctx):
    pos = jnp.arange(n_lat, dtype=jnp.int32)
    row = (pos // GRID_W).astype(F32)
    colp = (pos % GRID_W).astype(F32)
    n_freq = HEAD_DIM // 4
    inv_freq = ROPE_THETA ** (-jnp.arange(n_freq, dtype=F32) / n_freq)
    ang = jnp.concatenate([row[:, None] * inv_freq, colp[:, None] * inv_freq], axis=-1)
    cos, sin = jnp.cos(ang), jnp.sin(ang)
    cos_t = jnp.repeat(cos, 2, axis=-1)
    sin_t = jnp.stack([-sin, sin], axis=-1).reshape(n_lat, HEAD_DIM)
    cos_t = jnp.concatenate([cos_t, jnp.ones((n_ctx, HEAD_DIM), F32)], axis=0)
    sin_t = jnp.concatenate([sin_t, jnp.zeros((n_ctx, HEAD_DIM), F32)], axis=0)
    return cos_t, sin_t


def hybrid_layer(x, ctx, sb, lp, cos_t, sin_t, update_ctx):
    t = x.shape[0]
    lc = ctx.shape[0]
    layer = lp["layer"]
    mod = modulation(sb, lp["w_mod"], lp["b_mod"], layer)
    h = norm_modulate(x, ctx, lp["norm_w"], mod)
    p = in_proj(h, in_proj_weight(lp["w_in_t"], layer), lp["b_main"])
    dt_raw = dt_proj(h, lp["w_in_t"], lp["b_dt"], layer)

    u = ssd_conv(p, lp["conv_w"], lp["conv_b"], t // SSM_CHUNK)
    br_a = ssd_branch(u, dt_raw, lp["ssd_cst"], p, lp["d_exp"], lp["ssm_norm_w"], lc // SSM_CHUNK)

    qn, kn = qk_prep(p, cos_t, sin_t, lp["q_norm_w"], lp["k_norm_w"])
    tq = _pick(t, (1024, 512, 256))
    tk = _pick(t + lc, (1408, 768, 640, 512, 384, 256))
    br_b = flash_gqa(qn, kn, p, q_rows=t, q_row0=0, k_rows=t + lc, k_row0=0, tq=tq, tk=tk)

    br_c = neighbourhood(p, lp["na_bias"], t)

    s = gate_up(br_a, br_b, br_c, lp["w_up"], p, layer, rows=t, row0s=(0, 0, 0), p_row0=0)
    x_new = out_proj(s, lp["w_out"], x, mod, layer, 0)
    if not update_ctx:
        return x_new, ctx
    br_b_c = flash_gqa(qn, kn, p, q_rows=lc, q_row0=t, k_rows=lc, k_row0=t, tq=lc, tk=lc)
    s_c = gate_up(br_a, br_b_c, br_c, lp["w_up"], p, layer, rows=lc, row0s=(t, 0, t), p_row0=t)
    ctx_new = out_proj(s_c, lp["w_out"], ctx, mod, layer, 1)
    return x_new, ctx_new


def layer_params(layer, w_mod, b_mod, norm_w, w_in_t, b_in, conv_w, conv_b, a_log, dt_bias, d_skip, ssm_norm_w,
                 q_norm_w, k_norm_w, rpb, w_up, w_out):
    pad_heads = [(0, 0), (0, LANES - DT_W)]
    b = b_in[layer].astype(F32)
    cst = jnp.concatenate([
        jnp.pad(dt_bias[layer].astype(F32).reshape(1, DT_W), pad_heads),
        jnp.pad(-jnp.exp(a_log[layer].astype(F32)).reshape(1, DT_W), pad_heads),
        jnp.zeros((6, LANES), F32)], axis=0)
    return dict(
        layer=layer, w_mod=w_mod, b_mod=b_mod[layer], norm_w=norm_w[layer], w_in_t=w_in_t,
        b_main=jnp.concatenate([b[:_SRC_DT], b[_SRC_QA:]]),
        b_dt=jnp.pad(b[_SRC_DT:_SRC_QA], (0, LANES - DT_W)),
        conv_w=conv_w[layer], conv_b=conv_b[layer], ssd_cst=cst,
        d_exp=jnp.repeat(d_skip[layer].astype(F32), SSM_HEAD_DIM).reshape(1, BRANCH_W),
        ssm_norm_w=ssm_norm_w[layer],
        q_norm_w=q_norm_w[layer].reshape(1, HEAD_DIM), k_norm_w=k_norm_w[layer].reshape(1, HEAD_DIM),
        na_bias=na_bias_table(rpb[layer]), w_up=w_up, w_out=w_out)


def kernel(x, c, ctx, c_ctx, norm_w, w_mod, b_mod, w_in, b_in, conv_w, conv_b, a_log, dt_bias, d_skip, ssm_norm_w, q_norm_w, k_norm_w, rpb, w_up, w_out, final_norm_w):
    xs = x[0]
    cs = ctx[0]
    sb = jnp.broadcast_to(jnp.stack([c[0], c_ctx])[:, :, None], (2, D_MODEL, LANES))
    cos_t, sin_t = _rope_tables(xs.shape[0], cs.shape[0])
    w_in_t = jnp.swapaxes(w_in, 1, 2)
    w_up_b = w_up.astype(BF16)
    w_out_b = w_out.astype(BF16)
    for layer in range(DEPTH):
        lp = layer_params(layer, w_mod, b_mod, norm_w, w_in_t, b_in, conv_w, conv_b, a_log, dt_bias, d_skip,
                          ssm_norm_w, q_norm_w, k_norm_w, rpb, w_up_b, w_out_b)
        xs, cs = hybrid_layer(xs, cs, sb, lp, cos_t, sin_t, layer < DEPTH - 1)
    return final_norm(xs, final_norm_w)[None]
```

```python
import functools
import math

import numpy as np
import jax
import jax.numpy as jnp
from jax import lax
from jax.experimental import pallas as pl
from jax.experimental.pallas import tpu as pltpu

F32 = jnp.float32
BF16 = jnp.bfloat16

D_MODEL = 4096
DEPTH = 2
GRID_W = 64
N_BRANCH = 3
BRANCH_W = 2048
EPS = 1e-6

SSM_HEADS = 32
SSM_HEAD_DIM = 64
SSM_GROUPS = 4
SSM_STATE = 128
SSM_CHUNK = 128
CONV_K = 5
SSM_GN = SSM_GROUPS * SSM_STATE
XBC_W = BRANCH_W + 2 * SSM_GN
DT_W = 2 * SSM_HEADS
GROUP_W = (SSM_HEADS // SSM_GROUPS) * SSM_HEAD_DIM

ATT_HEADS = 16
KV_HEADS = 4
ATT_GROUP = ATT_HEADS // KV_HEADS
HEAD_DIM = 128
ROPE_THETA = 10000.0

NA_HEADS = 16
NA_WIN_R = 8
NA_WIN_C = 16

LANES = 128
MIB = 1024 * 1024

_SRC_DT = XBC_W + BRANCH_W
_SRC_QA = _SRC_DT + 2 * SSM_HEADS
_SRC_VA = _SRC_QA + BRANCH_W + KV_HEADS * HEAD_DIM
OFF_XBC = 0
OFF_Z = OFF_XBC + XBC_W
OFF_QA = OFF_Z + BRANCH_W
OFF_KA = OFF_QA + BRANCH_W
OFF_VA = OFF_KA + KV_HEADS * HEAD_DIM
OFF_GA = OFF_VA + KV_HEADS * HEAD_DIM
OFF_QN = OFF_GA + BRANCH_W
OFF_KN = OFF_QN + BRANCH_W
OFF_VN = OFF_KN + BRANCH_W
OFF_GN = OFF_VN + BRANCH_W
OFF_GM = OFF_GN + BRANCH_W
P_COLS = OFF_GM + N_BRANCH * D_MODEL
QK_W = BRANCH_W + KV_HEADS * HEAD_DIM

NEG_BIAS = -1e30


def _params(sem, vmem_mib):
    return pltpu.CompilerParams(dimension_semantics=sem, vmem_limit_bytes=vmem_mib * MIB)


def _pick(n, candidates):
    for c in candidates:
        if n % c == 0:
            return c
    raise ValueError(f"no tile for {n} in {candidates}")


def _sigmoid(v):
    return 1.0 / (1.0 + jnp.exp(-v))


def _silu(v):
    return v * _sigmoid(v)


def _mod_kernel(s_ref, w_ref, b_ref, o_ref):
    tn = w_ref.shape[1]
    k = w_ref.shape[0]
    o_ref[...] = jnp.zeros(o_ref.shape, F32)
    for v in range(2):
        s = _silu(s_ref[v])
        for jb in range(tn // LANES):
            sl = slice(jb * LANES, (jb + 1) * LANES)
            prod = w_ref[:, sl] * s
            part = prod.reshape(k // 8, 8, LANES).sum(axis=0)
            o_ref[v:v + 1, sl] = part.sum(axis=0, keepdims=True) + b_ref[:, sl]


def modulation(sb, w_mod, b_mod, layer):
    _, k, n = w_mod.shape
    tn = 512
    return pl.pallas_call(
        _mod_kernel,
        grid=(n // tn,),
        in_specs=[pl.BlockSpec((2, k, LANES), lambda j: (0, 0, 0)),
                  pl.BlockSpec((None, k, tn), lambda j: (layer, 0, j)),
                  pl.BlockSpec((1, tn), lambda j: (0, j))],
        out_specs=pl.BlockSpec((8, tn), lambda j: (0, j)),
        out_shape=jax.ShapeDtypeStruct((8, n), F32),
        compiler_params=_params(("arbitrary",), 40),
        name="modulation",
    )(sb, w_mod, b_mod.reshape(1, n))


def _norm_kernel(x_ref, c_ref, nw_ref, mod_ref, o_ref, *, n_lat_blocks):
    i = pl.program_id(0)

    def emit(src_ref, row):
        xv = src_ref[...]
        y = xv * lax.rsqrt(jnp.mean(xv * xv, axis=-1, keepdims=True) + EPS) * nw_ref[...]
        shift = mod_ref[row:row + 1, 0:D_MODEL]
        scale = mod_ref[row:row + 1, D_MODEL:2 * D_MODEL]
        o_ref[...] = (y * (1.0 + scale) + shift).astype(o_ref.dtype)

    @pl.when(i < n_lat_blocks)
    def _():
        emit(x_ref, 0)

    @pl.when(i >= n_lat_blocks)
    def _():
        emit(c_ref, 1)


def norm_modulate(x, ctx, norm_w, mod):
    t, d = x.shape
    lc = ctx.shape[0]
    tr = 256
    n_lat = t // tr
    n_ctx = lc // tr
    return pl.pallas_call(
        functools.partial(_norm_kernel, n_lat_blocks=n_lat),
        grid=(n_lat + n_ctx,),
        in_specs=[pl.BlockSpec((tr, d), lambda i: (jnp.minimum(i, n_lat - 1), 0)),
                  pl.BlockSpec((tr, d), lambda i: (jnp.maximum(i - n_lat, 0), 0)),
                  pl.BlockSpec((1, d), lambda i: (0, 0)),
                  pl.BlockSpec((8, 3 * d), lambda i: (0, 0))],
        out_specs=pl.BlockSpec((tr, d), lambda i: (i, 0)),
        out_shape=jax.ShapeDtypeStruct((t + lc, d), BF16),
        compiler_params=_params(("arbitrary",), 40),
        name="norm_modulate",
    )(x, ctx, norm_w.reshape(1, d), mod)


NT_DIMS = (((1,), (1,)), ((), ()))


def _inproj_kernel(h_ref, wa_ref, wb_ref, b_ref, o_ref, *, n_plain):
    def emit(w):
        acc = lax.dot_general(h_ref[...], w.astype(BF16), NT_DIMS, preferred_element_type=F32)
        o_ref[...] = (acc + b_ref[...]).astype(o_ref.dtype)

    j = pl.program_id(1)

    @pl.when(j < n_plain)
    def _():
        emit(wa_ref[...])

    @pl.when(j >= n_plain)
    def _():
        emit(jnp.concatenate([wa_ref[DT_W:, :], wb_ref[...]], axis=0))


def in_proj(h, w_t, b_main, layer):
    m, k = h.shape
    tm = _pick(m, (2816, 1280, 1024, 768, 640, 512, 256, 128))
    tn = 256
    return pl.pallas_call(
        functools.partial(_inproj_kernel, n_plain=_SRC_DT // tn),
        grid=(m // tm, P_COLS // tn),
        in_specs=[pl.BlockSpec((tm, k), lambda i, j: (i, 0), pipeline_mode=pl.Buffered(1)),
                  pl.BlockSpec((None, tn, k), lambda i, j: (layer, j, 0)),
                  pl.BlockSpec((None, DT_W, k), lambda i, j: (layer, (j + 1) * (tn // DT_W), 0)),
                  pl.BlockSpec((1, tn), lambda i, j: (0, j))],
        out_specs=pl.BlockSpec((tm, tn), lambda i, j: (i, j)),
        out_shape=jax.ShapeDtypeStruct((m, P_COLS), BF16),
        compiler_params=_params(("arbitrary", "arbitrary"), 52),
        name="in_proj",
    )(h, w_t, w_t, b_main.reshape(1, P_COLS))


def _dtproj_kernel(h_ref, w_ref, b_ref, o_ref):
    acc = lax.dot_general(h_ref[...], w_ref[...].astype(BF16), NT_DIMS, preferred_element_type=F32)
    o_ref[...] = acc + b_ref[...]


def dt_proj(h, w_t, b_dt, layer):
    m, k = h.shape
    tm = _pick(m, (1408, 1280, 1024, 768, 640, 512, 256, 128))
    return pl.pallas_call(
        _dtproj_kernel,
        grid=(m // tm,),
        in_specs=[pl.BlockSpec((tm, k), lambda i: (i, 0)),
                  pl.BlockSpec((None, LANES, k), lambda i: (layer, _SRC_DT // LANES, 0)),
                  pl.BlockSpec((1, LANES), lambda i: (0, 0))],
        out_specs=pl.BlockSpec((tm, LANES), lambda i: (i, 0)),
        out_shape=jax.ShapeDtypeStruct((m, LANES), F32),
        compiler_params=_params(("arbitrary",), 48),
        name="dt_proj",
    )(h, w_t, b_dt.reshape(1, LANES))


def _conv_kernel(prev_ref, cur_ref, next_ref, w_ref, b_ref, o_ref, *, n_lat_chunks, n_chunks):
    c = pl.program_id(0)
    has_prev = jnp.logical_and(c != 0, c != n_lat_chunks)
    has_next = jnp.logical_and(c != n_lat_chunks - 1, c != n_chunks - 1)
    pad = CONV_K // 2
    cw = 512
    for jb in range(XBC_W // cw):
        sl = slice(jb * cw, (jb + 1) * cw)
        cur = cur_ref[:, sl].astype(F32)
        pv = prev_ref[SSM_CHUNK - 16:SSM_CHUNK, sl].astype(F32)[8:16]
        nx = next_ref[0:16, sl].astype(F32)[0:8]
        pv = jnp.where(has_prev, pv, 0.0)
        nx = jnp.where(has_next, nx, 0.0)
        ext = jnp.concatenate([pv, cur, nx], axis=0)
        acc = jnp.broadcast_to(b_ref[:, sl], (SSM_CHUNK, cw))
        for kk in range(CONV_K):
            lo = 8 + kk - pad
            acc = acc + w_ref[kk:kk + 1, sl] * ext[lo:lo + SSM_CHUNK]
        o_ref[:, sl] = _silu(acc).astype(o_ref.dtype)


def ssd_conv(p, conv_w, conv_b, n_lat_chunks):
    r = p.shape[0]
    n_chunks = r // SSM_CHUNK

    def prev_map(c):
        bad = jnp.logical_or(c == 0, c == n_lat_chunks)
        return (jnp.where(bad, c, c - 1), 0)

    def next_map(c):
        bad = jnp.logical_or(c == n_lat_chunks - 1, c == n_chunks - 1)
        return (jnp.where(bad, c, c + 1), 0)

    blk = (SSM_CHUNK, XBC_W)
    return pl.pallas_call(
        functools.partial(_conv_kernel, n_lat_chunks=n_lat_chunks, n_chunks=n_chunks),
        grid=(n_chunks,),
        in_specs=[pl.BlockSpec(blk, prev_map),
                  pl.BlockSpec(blk, lambda c: (c, 0)),
                  pl.BlockSpec(blk, next_map),
                  pl.BlockSpec((8, XBC_W), lambda c: (0, 0)),
                  pl.BlockSpec((1, XBC_W), lambda c: (0, 0))],
        out_specs=pl.BlockSpec(blk, lambda c: (c, 0)),
        out_shape=jax.ShapeDtypeStruct((r, XBC_W), BF16),
        compiler_params=_params(("arbitrary",), 32),
        name="ssd_conv",
    )(p, p, p, jnp.pad(conv_w, ((0, 8 - CONV_K), (0, 0))), conv_b.reshape(1, XBC_W))


def _dot3(tri, v):
    hi = v.astype(BF16)
    r1 = v - hi.astype(F32)
    mid = r1.astype(BF16)
    lo = (r1 - mid.astype(F32)).astype(BF16)
    return (jnp.dot(tri, hi, preferred_element_type=F32)
            + jnp.dot(tri, mid, preferred_element_type=F32)
            + jnp.dot(tri, lo, preferred_element_type=F32))


def _pair_cols(v, ha, lane_lo):
    rows = v.shape[0]
    a = jnp.broadcast_to(v[:, ha:ha + 1], (rows, LANES))
    b = jnp.broadcast_to(v[:, ha + 1:ha + 2], (rows, LANES))
    return jnp.where(lane_lo, a, b)


SSD_STEP_CHUNKS = 1


def _ssd_chunk(u_ref, dt_ref, cst_ref, ht_ref, emit, rs, *, reverse):
    L = SSM_CHUNK
    hoff = SSM_HEADS if reverse else 0
    li = lax.broadcasted_iota(jnp.int32, (L, L), 0)
    si = lax.broadcasted_iota(jnp.int32, (L, L), 1)
    keep = (li <= si) if reverse else (li >= si)
    tri = jnp.where((si >= li) if reverse else (si <= li), 1.0, 0.0).astype(BF16)
    lane_lo = lax.broadcasted_iota(jnp.int32, (L, LANES), 1) < SSM_HEAD_DIM
    lane_lo1 = lax.broadcasted_iota(jnp.int32, (1, LANES), 1) < SSM_HEAD_DIM

    raw = dt_ref[rs, :] + cst_ref[0:1, :]
    dtv = jnp.maximum(raw, 0.0) + jnp.log1p(jnp.exp(-jnp.abs(raw)))
    da = dtv * cst_ref[1:2, :]
    acs = _dot3(tri, da)
    acs_t = acs.T
    dt_t = dtv.T
    a_last = acs[0:1, :] if reverse else acs[L - 1:L, :]
    wst = jnp.exp(a_last - acs) * dtv
    eacs = jnp.exp(acs)
    e_last = jnp.exp(a_last)

    pairs = GROUP_W // LANES
    for g in range(SSM_GROUPS):
        bg = u_ref[rs, BRANCH_W + g * SSM_STATE:BRANCH_W + (g + 1) * SSM_STATE]
        cg = u_ref[rs, BRANCH_W + SSM_GN + g * SSM_STATE:BRANCH_W + SSM_GN + (g + 1) * SSM_STATE]
        cb = lax.dot_general(cg, bg, (((1,), (1,)), ((), ())), preferred_element_type=F32)
        bg_t = bg.astype(F32).T.astype(BF16)
        y_parts, xw_parts, e_parts, d_parts = [], [], [], []
        for pi in range(pairs):
            ha = hoff + g * (2 * pairs) + 2 * pi
            xp = u_ref[rs, g * GROUP_W + pi * LANES:g * GROUP_W + (pi + 1) * LANES]
            zero = jnp.zeros_like(xp)
            y_pair = None
            for hh, xm in ((ha, jnp.where(lane_lo, xp, zero)), (ha + 1, jnp.where(lane_lo, zero, xp))):
                diff = acs[:, hh:hh + 1] - acs_t[hh:hh + 1, :]
                dec = jnp.exp(jnp.where(keep, diff, -jnp.inf))
                m = (cb * dec * dt_t[hh:hh + 1, :]).astype(BF16)
                yh = jnp.dot(m, xm, preferred_element_type=F32)
                y_pair = yh if y_pair is None else y_pair + yh
            y_parts.append(y_pair)
            xw_parts.append((xp.astype(F32) * _pair_cols(wst, ha, lane_lo)).astype(BF16))
            e_parts.append(_pair_cols(eacs, ha, lane_lo))
            d_parts.append(_pair_cols(e_last, ha, lane_lo1))
        hs = ht_ref[g]
        y_off = jnp.dot(cg, hs.astype(BF16), preferred_element_type=F32) * jnp.concatenate(e_parts, axis=1)
        s_new = jnp.dot(bg_t, jnp.concatenate(xw_parts, axis=1), preferred_element_type=F32)
        ht_ref[g] = hs * jnp.concatenate(d_parts, axis=1) + s_new
        emit(g, jnp.concatenate(y_parts, axis=1) + y_off)


def _ssd_fwd_kernel(u_ref, dt_ref, cst_ref, o_ref, ht_ref):
    @pl.when(pl.program_id(0) == 0)
    def _():
        ht_ref[...] = jnp.zeros(ht_ref.shape, F32)

    for ci in range(SSD_STEP_CHUNKS):
        rs = slice(ci * SSM_CHUNK, (ci + 1) * SSM_CHUNK)

        def emit(g, y, rs=rs):
            o_ref[rs, g * GROUP_W:(g + 1) * GROUP_W] = y

        _ssd_chunk(u_ref, dt_ref, cst_ref, ht_ref, emit, rs, reverse=False)


def _ssd_bwd_kernel(u_ref, dt_ref, cst_ref, yf_ref, z0_ref, z1_ref, dsk_ref, nw_ref, o_ref, ht_ref, y_ref):
    @pl.when(pl.program_id(0) == 0)
    def _():
        ht_ref[...] = jnp.zeros(ht_ref.shape, F32)

    half = SSM_GROUPS // 2

    for ci in reversed(range(SSD_STEP_CHUNKS)):
        rs = slice(ci * SSM_CHUNK, (ci + 1) * SSM_CHUNK)

        def emit(g, y, rs=rs):
            sl = slice(g * GROUP_W, (g + 1) * GROUP_W)
            z_ref = z0_ref if g < half else z1_ref
            zsl = slice((g % half) * GROUP_W, (g % half + 1) * GROUP_W)
            tot = y + yf_ref[rs, sl] + dsk_ref[:, sl] * u_ref[rs, sl].astype(F32)
            y_ref[rs, sl] = tot * _silu(z_ref[rs, zsl].astype(F32))

        _ssd_chunk(u_ref, dt_ref, cst_ref, ht_ref, emit, rs, reverse=True)
    yv = y_ref[...]
    o_ref[...] = (yv * lax.rsqrt(jnp.mean(yv * yv, axis=-1, keepdims=True) + EPS) * nw_ref[...]).astype(o_ref.dtype)


def ssd_branch(u, dt_raw, cst, p, d_exp, ssm_norm_w, n_ctx_rows):
    r = u.shape[0]
    step_rows = SSD_STEP_CHUNKS * SSM_CHUNK
    n_chunks = r // step_rows
    n_ctx_chunks = n_ctx_rows // step_rows
    n_lat = n_chunks - n_ctx_chunks
    state = pltpu.VMEM((SSM_GROUPS, SSM_STATE, GROUP_W), F32)

    def fwd_idx(i):
        return jnp.where(i < n_ctx_chunks, n_lat + i, i - n_ctx_chunks)

    def bwd_idx(i):
        return n_chunks - 1 - i

    ublk = (step_rows, XBC_W)
    yf = pl.pallas_call(
        _ssd_fwd_kernel,
        grid=(n_chunks,),
        in_specs=[pl.BlockSpec(ublk, lambda i: (fwd_idx(i), 0)),
                  pl.BlockSpec((step_rows, LANES), lambda i: (fwd_idx(i), 0)),
                  pl.BlockSpec((8, LANES), lambda i: (0, 0))],
        out_specs=pl.BlockSpec((step_rows, BRANCH_W), lambda i: (fwd_idx(i), 0)),
        out_shape=jax.ShapeDtypeStruct((r, BRANCH_W), F32),
        scratch_shapes=[state],
        compiler_params=_params(("arbitrary",), 32),
        name="ssd_fwd",
    )(u, dt_raw, cst)

    zw = BRANCH_W // 2
    z0 = OFF_Z // zw
    return pl.pallas_call(
        _ssd_bwd_kernel,
        grid=(n_chunks,),
        in_specs=[pl.BlockSpec(ublk, lambda i: (bwd_idx(i), 0)),
                  pl.BlockSpec((step_rows, LANES), lambda i: (bwd_idx(i), 0)),
                  pl.BlockSpec((8, LANES), lambda i: (0, 0)),
                  pl.BlockSpec((step_rows, BRANCH_W), lambda i: (bwd_idx(i), 0)),
                  pl.BlockSpec((step_rows, zw), lambda i: (bwd_idx(i), z0)),
                  pl.BlockSpec((step_rows, zw), lambda i: (bwd_idx(i), z0 + 1)),
                  pl.BlockSpec((1, BRANCH_W), lambda i: (0, 0)),
                  pl.BlockSpec((1, BRANCH_W), lambda i: (0, 0))],
        out_specs=pl.BlockSpec((step_rows, BRANCH_W), lambda i: (bwd_idx(i), 0)),
        out_shape=jax.ShapeDtypeStruct((r, BRANCH_W), BF16),
        scratch_shapes=[state, pltpu.VMEM((step_rows, BRANCH_W), F32)],
        compiler_params=_params(("arbitrary",), 32),
        name="ssd_bwd",
    )(u, dt_raw, cst, yf, p, p, d_exp, ssm_norm_w.reshape(1, BRANCH_W))


def _qkprep_kernel(qk_ref, cos_ref, sin_ref, qw_ref, kw_ref, q_ref, k_ref):
    cosv = cos_ref[...]
    sinv = sin_ref[...]
    scale = HEAD_DIM ** -0.5 * math.log2(math.e)
    even = lax.broadcasted_iota(jnp.int32, cosv.shape, 1) % 2 == 0
    for h in range(ATT_HEADS + KV_HEADS):
        t = qk_ref[:, h * HEAD_DIM:(h + 1) * HEAD_DIM].astype(F32)
        w = qw_ref[...] if h < ATT_HEADS else kw_ref[...]
        n = t * lax.rsqrt(jnp.mean(t * t, axis=-1, keepdims=True) + EPS) * w
        partner = jnp.where(even, pltpu.roll(n, HEAD_DIM - 1, 1), pltpu.roll(n, 1, 1))
        rot = n * cosv + partner * sinv
        if h < ATT_HEADS:
            q_ref[:, h * HEAD_DIM:(h + 1) * HEAD_DIM] = (rot * scale).astype(q_ref.dtype)
        else:
            hk = h - ATT_HEADS
            k_ref[:, hk * HEAD_DIM:(hk + 1) * HEAD_DIM] = rot.astype(k_ref.dtype)


def qk_prep(p, cos_t, sin_t, qw, kw):
    r = p.shape[0]
    tr = 256
    kvw = KV_HEADS * HEAD_DIM
    return pl.pallas_call(
        _qkprep_kernel,
        grid=(r // tr,),
        in_specs=[pl.BlockSpec((tr, QK_W), lambda i: (i, OFF_QA // QK_W)),
                  pl.BlockSpec((tr, HEAD_DIM), lambda i: (i, 0)),
                  pl.BlockSpec((tr, HEAD_DIM), lambda i: (i, 0)),
                  pl.BlockSpec((1, HEAD_DIM), lambda i: (0, 0)),
                  pl.BlockSpec((1, HEAD_DIM), lambda i: (0, 0))],
        out_specs=[pl.BlockSpec((tr, BRANCH_W), lambda i: (i, 0)),
                   pl.BlockSpec((tr, kvw), lambda i: (i, 0))],
        out_shape=[jax.ShapeDtypeStruct((r, BRANCH_W), BF16),
                   jax.ShapeDtypeStruct((r, kvw), BF16)],
        compiler_params=_params(("arbitrary",), 32),
        name="qk_prep",
    )(p, cos_t, sin_t, qw, kw)


def _flash_kernel(q_ref, k_ref, v_ref, g_ref, o_ref, m_ref, l_ref, acc_ref, *, nk):
    ki = pl.program_id(2)

    @pl.when(ki == 0)
    def _():
        m_ref[...] = jnp.full(m_ref.shape, -jnp.inf, F32)
        l_ref[...] = jnp.zeros(l_ref.shape, F32)
        acc_ref[...] = jnp.zeros(acc_ref.shape, F32)

    k = k_ref[...]
    v1 = jnp.concatenate([v_ref[...], jnp.ones(v_ref.shape, v_ref.dtype)], axis=1)
    reps = k.shape[0] // LANES
    for r in range(ATT_GROUP):
        hs = slice(r * HEAD_DIM, (r + 1) * HEAD_DIM)
        s = lax.dot_general(q_ref[:, hs], k, NT_DIMS, preferred_element_type=F32)
        m_prev = m_ref[r]
        m_next = jnp.maximum(m_prev, jnp.max(s, axis=1, keepdims=True))
        p = jnp.exp2(s - jnp.concatenate([m_next] * reps, axis=1))
        alpha = jnp.exp2(m_prev - m_next)
        pv = jnp.dot(p.astype(v1.dtype), v1, preferred_element_type=F32)
        l_ref[r] = alpha * l_ref[r] + pv[:, HEAD_DIM:]
        acc_ref[r] = alpha * acc_ref[r] + pv[:, :HEAD_DIM]
        m_ref[r] = m_next

    @pl.when(ki == nk - 1)
    def _():
        for r in range(ATT_GROUP):
            hs = slice(r * HEAD_DIM, (r + 1) * HEAD_DIM)
            o = acc_ref[r] / l_ref[r]
            o_ref[:, hs] = (_silu(g_ref[:, hs].astype(F32)) * o).astype(o_ref.dtype)


def flash_gqa(qn, kn, p, *, q_rows, q_row0, k_rows, k_row0, tq, tk):
    nq = q_rows // tq
    nk = k_rows // tk
    qb0 = q_row0 // tq
    kb0 = k_row0 // tk
    gw = ATT_GROUP * HEAD_DIM
    return pl.pallas_call(
        functools.partial(_flash_kernel, nk=nk),
        grid=(KV_HEADS, nq, nk),
        in_specs=[pl.BlockSpec((tq, gw), lambda g, i, j: (qb0 + i, g)),
                  pl.BlockSpec((tk, HEAD_DIM), lambda g, i, j: (kb0 + j, g)),
                  pl.BlockSpec((tk, HEAD_DIM), lambda g, i, j: (kb0 + j, OFF_VA // HEAD_DIM + g)),
                  pl.BlockSpec((tq, gw), lambda g, i, j: (qb0 + i, OFF_GA // gw + g))],
        out_specs=pl.BlockSpec((tq, gw), lambda g, i, j: (i, g)),
        out_shape=jax.ShapeDtypeStruct((q_rows, BRANCH_W), BF16),
        scratch_shapes=[pltpu.VMEM((ATT_GROUP, tq, LANES), F32),
                        pltpu.VMEM((ATT_GROUP, tq, LANES), F32),
                        pltpu.VMEM((ATT_GROUP, tq, HEAD_DIM), F32)],
        compiler_params=_params(("arbitrary", "arbitrary", "arbitrary"), 48),
        name="flash_gqa",
    )(qn, kn, p, p)


NA_QROWS = 4
NA_KROWS = 12
NA_PATTERNS = 3


def _na_rel_row(pattern, qr, kr):
    if pattern == 0:
        return kr - qr + (NA_WIN_R - 1) if kr < NA_WIN_R else None
    if pattern == 1:
        return kr - qr + (NA_WIN_R - 1 - NA_WIN_R // 2) if 0 <= kr - qr < NA_WIN_R else None
    first = NA_KROWS - NA_WIN_R
    return kr - qr + (NA_WIN_R - 1) - (NA_KROWS - NA_QROWS) if kr >= first else None


def _na_kernel(q_ref, k_ref, v_ref, g_ref, tp_ref, o_ref, bias_ref, *, n_lat, n_ctx, grid_rows):
    scale = HEAD_DIM ** -0.5
    qn = NA_QROWS * GRID_W
    win = NA_KROWS * GRID_W
    n_blocks = grid_rows // NA_QROWS
    kc = k_ref[n_lat:n_lat + n_ctx, :]
    vc = v_ref[n_lat:n_lat + n_ctx, :]
    nt = (((1,), (1,)), ((), ()))

    lane_lo = lax.broadcasted_iota(jnp.int32, (GRID_W, LANES), 1) < GRID_W
    for pattern in range(NA_PATTERNS):
        for qr in range(NA_QROWS):
            for kp in range(NA_KROWS // 2):
                ja = _na_rel_row(pattern, qr, 2 * kp)
                jb = _na_rel_row(pattern, qr, 2 * kp + 1)
                if ja is None and jb is None:
                    tile = jnp.full((GRID_W, LANES), NEG_BIAS, F32)
                elif jb is None:
                    tile = jnp.where(lane_lo, tp_ref[ja + 1], NEG_BIAS)
                elif ja is None:
                    tile = jnp.where(lane_lo, NEG_BIAS, tp_ref[jb])
                else:
                    tile = tp_ref[jb]
                bias_ref[pattern, qr * GRID_W:(qr + 1) * GRID_W, kp * LANES:(kp + 1) * LANES] = tile

    def block_body(b, carry):
        u0 = jnp.clip(b * NA_QROWS - NA_WIN_R // 2, 0, grid_rows - NA_KROWS)
        pattern = jnp.where(b == 0, 0, jnp.where(b == n_blocks - 1, 2, 1))
        qs = pl.ds(pl.multiple_of(b * qn, qn), qn)
        ws = pl.ds(pl.multiple_of(u0 * GRID_W, GRID_W), win)
        q = q_ref[qs, :]
        s_loc = lax.dot_general(q, k_ref[ws, :], nt, preferred_element_type=F32) * scale + bias_ref[pattern]
        s_ctx = lax.dot_general(q, kc, nt, preferred_element_type=F32) * scale
        m = jnp.maximum(jnp.max(s_loc, axis=1, keepdims=True), jnp.max(s_ctx, axis=1, keepdims=True))
        p_loc = jnp.exp(s_loc - m)
        p_ctx = jnp.exp(s_ctx - m)
        den = jnp.sum(p_loc, axis=1, keepdims=True) + jnp.sum(p_ctx, axis=1, keepdims=True)
        o = (jnp.dot(p_loc.astype(BF16), v_ref[ws, :], preferred_element_type=F32)
             + jnp.dot(p_ctx.astype(BF16), vc, preferred_element_type=F32)) / den
        o_ref[qs, :] = (_silu(g_ref[qs, :].astype(F32)) * o).astype(o_ref.dtype)
        return carry

    lax.fori_loop(0, n_blocks, block_body, 0, unroll=2)

    qs = slice(n_lat, n_lat + n_ctx)
    s = lax.dot_general(q_ref[qs, :], kc, nt, preferred_element_type=F32) * scale
    pc = jnp.exp(s - jnp.max(s, axis=1, keepdims=True))
    oc = jnp.dot(pc.astype(BF16), vc, preferred_element_type=F32) / jnp.sum(pc, axis=1, keepdims=True)
    o_ref[qs, :] = (_silu(g_ref[qs, :].astype(F32)) * oc).astype(o_ref.dtype)


def neighbourhood(p, tp, n_lat):
    r = p.shape[0]
    n_ctx = r - n_lat
    col = lambda off: (lambda h: (0, off // HEAD_DIM + h))
    return pl.pallas_call(
        functools.partial(_na_kernel, n_lat=n_lat, n_ctx=n_ctx, grid_rows=n_lat // GRID_W),
        grid=(NA_HEADS,),
        in_specs=[pl.BlockSpec((r, HEAD_DIM), col(OFF_QN)),
                  pl.BlockSpec((r, HEAD_DIM), col(OFF_KN)),
                  pl.BlockSpec((r, HEAD_DIM), col(OFF_VN)),
                  pl.BlockSpec((r, HEAD_DIM), col(OFF_GN)),
                  pl.BlockSpec((None, 2 * NA_WIN_R, GRID_W, 2 * GRID_W), lambda h: (h, 0, 0, 0))],
        out_specs=pl.BlockSpec((r, HEAD_DIM), lambda h: (0, h)),
        out_shape=jax.ShapeDtypeStruct((r, BRANCH_W), BF16),
        scratch_shapes=[pltpu.VMEM((NA_PATTERNS, NA_QROWS * GRID_W, NA_KROWS * GRID_W), F32)],
        compiler_params=_params(("arbitrary",), 48),
        name="neighbourhood",
    )(p, p, p, p, tp)


def na_bias_table(rpb):
    n_rel = 2 * NA_WIN_C - 1
    c = np.arange(GRID_W)[:, None]
    kc = np.arange(GRID_W)[None, :]
    c0 = np.clip(c - NA_WIN_C // 2, 0, GRID_W - NA_WIN_C)
    valid = (kc >= c0) & (kc < c0 + NA_WIN_C)
    rel = kc - c + (NA_WIN_C - 1)
    onehot = (rel[None] == np.arange(n_rel)[:, None, None]) & valid[None]
    h, nr, _ = rpb.shape
    t = jnp.dot(rpb.astype(F32).reshape(h * nr, n_rel), jnp.asarray(onehot.reshape(n_rel, -1), F32),
                precision=lax.Precision.HIGHEST).reshape(h, nr, GRID_W, GRID_W)
    t = jnp.where(jnp.asarray(valid), t, NEG_BIAS)
    masked = jnp.full((h, 1, GRID_W, GRID_W), NEG_BIAS, F32)
    tpad = jnp.concatenate([masked, t, masked], axis=1)
    return jnp.concatenate([tpad[:, :-1], tpad[:, 1:]], axis=-1)


def _gateup_kernel(a_ref, b_ref, c_ref, wa_ref, wb_ref, wc_ref, ga_ref, gb_ref, gc_ref, o_ref):
    acc = None
    for br_ref, w_ref, g_ref in ((a_ref, wa_ref, ga_ref), (b_ref, wb_ref, gb_ref), (c_ref, wc_ref, gc_ref)):
        up = jnp.dot(br_ref[...], w_ref[...], preferred_element_type=F32)
        term = _sigmoid(g_ref[...].astype(F32)) * up
        acc = term if acc is None else acc + term
    o_ref[...] = acc.astype(o_ref.dtype)


def gate_up(br_a, br_b, br_c, w_up, p, layer, *, rows, row0s, p_row0):
    tm = _pick(rows, (1024, 512, 256))
    tn = 512

    def br_spec(row0):
        return pl.BlockSpec((tm, BRANCH_W), lambda i, j: (row0 // tm + i, 0))

    def w_spec(b):
        return pl.BlockSpec((None, None, BRANCH_W, tn), lambda i, j: (layer, b, 0, j))

    def g_spec(b):
        return pl.BlockSpec((tm, tn), lambda i, j: (p_row0 // tm + i, (OFF_GM + b * D_MODEL) // tn + j))

    return pl.pallas_call(
        _gateup_kernel,
        grid=(rows // tm, D_MODEL // tn),
        in_specs=[br_spec(row0s[0]), br_spec(row0s[1]), br_spec(row0s[2]),
                  w_spec(0), w_spec(1), w_spec(2), g_spec(0), g_spec(1), g_spec(2)],
        out_specs=pl.BlockSpec((tm, tn), lambda i, j: (i, j)),
        out_shape=jax.ShapeDtypeStruct((rows, D_MODEL), BF16),
        compiler_params=_params(("arbitrary", "arbitrary"), 56),
        name="gate_up",
    )(br_a, br_b, br_c, w_up, w_up, w_up, p, p, p)


def _outproj_kernel(s_ref, w_ref, x_ref, mod_ref, o_ref, *, mod_row):
    y = jnp.dot(s_ref[...], w_ref[...], preferred_element_type=F32)
    o_ref[...] = x_ref[...] + mod_ref[mod_row:mod_row + 1, :] * y


def out_proj(s, w_out, x, mod, layer, mod_row):
    rows = s.shape[0]
    tm = _pick(rows, (1024, 512, 256))
    tn = 512
    return pl.pallas_call(
        functools.partial(_outproj_kernel, mod_row=mod_row),
        grid=(rows // tm, D_MODEL // tn),
        in_specs=[pl.BlockSpec((tm, D_MODEL), lambda i, j: (i, 0)),
                  pl.BlockSpec((None, D_MODEL, tn), lambda i, j: (layer, 0, j)),
                  pl.BlockSpec((tm, tn), lambda i, j: (i, j)),
                  pl.BlockSpec((8, tn), lambda i, j: (0, 2 * D_MODEL // tn + j))],
        out_specs=pl.BlockSpec((tm, tn), lambda i, j: (i, j)),
        out_shape=jax.ShapeDtypeStruct((rows, D_MODEL), F32),
        compiler_params=_params(("arbitrary", "arbitrary"), 48),
        name="out_proj",
    )(s, w_out, x, mod)


def _final_norm_kernel(x_ref, w_ref, o_ref):
    xv = x_ref[...]
    o_ref[...] = xv * lax.rsqrt(jnp.mean(xv * xv, axis=-1, keepdims=True) + EPS) * w_ref[...]


def final_norm(x, w):
    t, d = x.shape
    tr = 256
    return pl.pallas_call(
        _final_norm_kernel,
        grid=(t // tr,),
        in_specs=[pl.BlockSpec((tr, d), lambda i: (i, 0)), pl.BlockSpec((1, d), lambda i: (0, 0))],
        out_specs=pl.BlockSpec((tr, d), lambda i: (i, 0)),
        out_shape=jax.ShapeDtypeStruct((t, d), F32),
        compiler_params=_params(("arbitrary",), 40),
        name="final_norm",
    )(x, w.reshape(1, d))


def _rope_tables(n_lat, n_ctx):
    pos = jnp.arange(n_lat, dtype=jnp.int32)
    row = (pos // GRID_W).astype(F32)
    colp = (pos % GRID_W).astype(F32)
    n_freq = HEAD_DIM // 4
    inv_freq = ROPE_THETA ** (-jnp.arange(n_freq, dtype=F32) / n_freq)
    ang = jnp.concatenate([row[:, None] * inv_freq, colp[:, None] * inv_freq], axis=-1)
    cos, sin = jnp.cos(ang), jnp.sin(ang)
    cos_t = jnp.repeat(cos, 2, axis=-1)
    sin_t = jnp.stack([-sin, sin], axis=-1).reshape(n_lat, HEAD_DIM)
    cos_t = jnp.concatenate([cos_t, jnp.ones((n_ctx, HEAD_DIM), F32)], axis=0)
    sin_t = jnp.concatenate([sin_t, jnp.zeros((n_ctx, HEAD_DIM), F32)], axis=0)
    return cos_t, sin_t


def hybrid_layer(x, ctx, sb, lp, cos_t, sin_t, update_ctx):
    t = x.shape[0]
    lc = ctx.shape[0]
    layer = lp["layer"]
    mod = modulation(sb, lp["w_mod"], lp["b_mod"], layer)
    h = norm_modulate(x, ctx, lp["norm_w"], mod)
    p = in_proj(h, lp["w_in_t"], lp["b_main"], layer)
    dt_raw = dt_proj(h, lp["w_in_t"], lp["b_dt"], layer)

    u = ssd_conv(p, lp["conv_w"], lp["conv_b"], t // SSM_CHUNK)
    br_a = ssd_branch(u, dt_raw, lp["ssd_cst"], p, lp["d_exp"], lp["ssm_norm_w"], lc)

    qn, kn = qk_prep(p, cos_t, sin_t, lp["q_norm_w"], lp["k_norm_w"])
    tq = _pick(t, (1024, 512, 256))
    tk = _pick(t + lc, (1408, 768, 640, 512, 384, 256))
    br_b = flash_gqa(qn, kn, p, q_rows=t, q_row0=0, k_rows=t + lc, k_row0=0, tq=tq, tk=tk)

    br_c = neighbourhood(p, lp["na_bias"], t)

    s = gate_up(br_a, br_b, br_c, lp["w_up"], p, layer, rows=t, row0s=(0, 0, 0), p_row0=0)
    x_new = out_proj(s, lp["w_out"], x, mod, layer, 0)
    if not update_ctx:
        return x_new, ctx
    br_b_c = flash_gqa(qn, kn, p, q_rows=lc, q_row0=t, k_rows=lc, k_row0=t, tq=lc, tk=lc)
    s_c = gate_up(br_a, br_b_c, br_c, lp["w_up"], p, layer, rows=lc, row0s=(t, 0, t), p_row0=t)
    ctx_new = out_proj(s_c, lp["w_out"], ctx, mod, layer, 1)
    return x_new, ctx_new


def layer_params(layer, w_mod, b_mod, norm_w, w_in_t, b_in, conv_w, conv_b, a_log, dt_bias, d_skip, ssm_norm_w,
                 q_norm_w, k_norm_w, rpb, w_up, w_out):
    pad_heads = [(0, 0), (0, LANES - DT_W)]
    b = b_in[layer].astype(F32)
    cst = jnp.concatenate([
        jnp.pad(dt_bias[layer].astype(F32).reshape(1, DT_W), pad_heads),
        jnp.pad(-jnp.exp(a_log[layer].astype(F32)).reshape(1, DT_W), pad_heads),
        jnp.zeros((6, LANES), F32)], axis=0)
    return dict(
        layer=layer, w_mod=w_mod, b_mod=b_mod[layer], norm_w=norm_w[layer], w_in_t=w_in_t,
        b_main=jnp.concatenate([b[:_SRC_DT], b[_SRC_QA:]]),
        b_dt=jnp.pad(b[_SRC_DT:_SRC_QA], (0, LANES - DT_W)),
        conv_w=conv_w[layer], conv_b=conv_b[layer], ssd_cst=cst,
        d_exp=jnp.repeat(d_skip[layer].astype(F32), SSM_HEAD_DIM).reshape(1, BRANCH_W),
        ssm_norm_w=ssm_norm_w[layer],
        q_norm_w=q_norm_w[layer].reshape(1, HEAD_DIM), k_norm_w=k_norm_w[layer].reshape(1, HEAD_DIM),
        na_bias=na_bias_table(rpb[layer]), w_up=w_up, w_out=w_out)


def kernel(x, c, ctx, c_ctx, norm_w, w_mod, b_mod, w_in, b_in, conv_w, conv_b, a_log, dt_bias, d_skip, ssm_norm_w, q_norm_w, k_norm_w, rpb, w_up, w_out, final_norm_w):
    xs = x[0]
    cs = ctx[0]
    sb = jnp.broadcast_to(jnp.stack([c[0], c_ctx])[:, :, None], (2, D_MODEL, LANES))
    cos_t, sin_t = _rope_tables(xs.shape[0], cs.shape[0])
    w_in_t = jnp.swapaxes(w_in, 1, 2)
    w_up_b = w_up.astype(BF16)
    w_out_b = w_out.astype(BF16)
    for layer in range(DEPTH):
        lp = layer_params(layer, w_mod, b_mod, norm_w, w_in_t, b_in, conv_w, conv_b, a_log, dt_bias, d_skip,
                          ssm_norm_w, q_norm_w, k_norm_w, rpb, w_up_b, w_out_b)
        xs, cs = hybrid_layer(xs, cs, sb, lp, cos_t, sin_t, layer < DEPTH - 1)
    return final_norm(xs, final_norm_w)[None]
```

```python
import functools
import math

import numpy as np
import jax
import jax.numpy as jnp
from jax import lax
from jax.experimental import pallas as pl
from jax.experimental.pallas import tpu as pltpu

F32 = jnp.float32
BF16 = jnp.bfloat16

D_MODEL = 4096
DEPTH = 2
GRID_W = 64
N_BRANCH = 3
BRANCH_W = 2048
EPS = 1e-6

SSM_HEADS = 32
SSM_HEAD_DIM = 64
SSM_GROUPS = 4
SSM_STATE = 128
SSM_CHUNK = 128
CONV_K = 5
SSM_GN = SSM_GROUPS * SSM_STATE
XBC_W = BRANCH_W + 2 * SSM_GN
DT_W = 2 * SSM_HEADS
GROUP_W = (SSM_HEADS // SSM_GROUPS) * SSM_HEAD_DIM

ATT_HEADS = 16
KV_HEADS = 4
ATT_GROUP = ATT_HEADS // KV_HEADS
HEAD_DIM = 128
ROPE_THETA = 10000.0

NA_HEADS = 16
NA_WIN_R = 8
NA_WIN_C = 16

LANES = 128
MIB = 1024 * 1024

_SRC_DT = XBC_W + BRANCH_W
_SRC_QA = _SRC_DT + 2 * SSM_HEADS
_SRC_VA = _SRC_QA + BRANCH_W + KV_HEADS * HEAD_DIM
OFF_XBC = 0
OFF_Z = OFF_XBC + XBC_W
OFF_QA = OFF_Z + BRANCH_W
OFF_KA = OFF_QA + BRANCH_W
OFF_VA = OFF_KA + KV_HEADS * HEAD_DIM
OFF_GA = OFF_VA + KV_HEADS * HEAD_DIM
OFF_QN = OFF_GA + BRANCH_W
OFF_KN = OFF_QN + BRANCH_W
OFF_VN = OFF_KN + BRANCH_W
OFF_GN = OFF_VN + BRANCH_W
OFF_GM = OFF_GN + BRANCH_W
P_COLS = OFF_GM + N_BRANCH * D_MODEL
QK_W = BRANCH_W + KV_HEADS * HEAD_DIM

NEG_BIAS = -1e30
LOG2E = math.log2(math.e)


def _params(sem, vmem_mib):
    return pltpu.CompilerParams(dimension_semantics=sem, vmem_limit_bytes=vmem_mib * MIB)


def _pick(n, candidates):
    for c in candidates:
        if n % c == 0:
            return c
    raise ValueError(f"no tile for {n} in {candidates}")


def _sigmoid(v):
    return 1.0 / (1.0 + jnp.exp(-v))


def _silu(v):
    return v * _sigmoid(v)


def _mod_kernel(s_ref, w_ref, b_ref, o_ref):
    tn = w_ref.shape[1]
    k = w_ref.shape[0]
    o_ref[...] = jnp.zeros(o_ref.shape, F32)
    for v in range(2):
        s = _silu(s_ref[v])
        for jb in range(tn // LANES):
            sl = slice(jb * LANES, (jb + 1) * LANES)
            prod = w_ref[:, sl] * s
            part = prod.reshape(k // 8, 8, LANES).sum(axis=0)
            o_ref[v:v + 1, sl] = part.sum(axis=0, keepdims=True) + b_ref[:, sl]


def modulation(sb, w_mod, b_mod, layer):
    _, k, n = w_mod.shape
    tn = 512
    return pl.pallas_call(
        _mod_kernel,
        grid=(n // tn,),
        in_specs=[pl.BlockSpec((2, k, LANES), lambda j: (0, 0, 0)),
                  pl.BlockSpec((None, k, tn), lambda j: (layer, 0, j)),
                  pl.BlockSpec((1, tn), lambda j: (0, j))],
        out_specs=pl.BlockSpec((8, tn), lambda j: (0, j)),
        out_shape=jax.ShapeDtypeStruct((8, n), F32),
        compiler_params=_params(("arbitrary",), 40),
        name="modulation",
    )(sb, w_mod, b_mod.reshape(1, n))


def _norm_kernel(x_ref, c_ref, nw_ref, mod_ref, o_ref, *, n_lat_blocks):
    i = pl.program_id(0)

    def emit(src_ref, row):
        xv = src_ref[...]
        y = xv * lax.rsqrt(jnp.mean(xv * xv, axis=-1, keepdims=True) + EPS) * nw_ref[...]
        shift = mod_ref[row:row + 1, 0:D_MODEL]
        scale = mod_ref[row:row + 1, D_MODEL:2 * D_MODEL]
        o_ref[...] = (y * (1.0 + scale) + shift).astype(o_ref.dtype)

    @pl.when(i < n_lat_blocks)
    def _():
        emit(x_ref, 0)

    @pl.when(i >= n_lat_blocks)
    def _():
        emit(c_ref, 1)


def norm_modulate(x, ctx, norm_w, mod):
    t, d = x.shape
    lc = ctx.shape[0]
    tr = 256
    n_lat = t // tr
    n_ctx = lc // tr
    return pl.pallas_call(
        functools.partial(_norm_kernel, n_lat_blocks=n_lat),
        grid=(n_lat + n_ctx,),
        in_specs=[pl.BlockSpec((tr, d), lambda i: (jnp.minimum(i, n_lat - 1), 0)),
                  pl.BlockSpec((tr, d), lambda i: (jnp.maximum(i - n_lat, 0), 0)),
                  pl.BlockSpec((1, d), lambda i: (0, 0)),
                  pl.BlockSpec((8, 3 * d), lambda i: (0, 0))],
        out_specs=pl.BlockSpec((tr, d), lambda i: (i, 0)),
        out_shape=jax.ShapeDtypeStruct((t + lc, d), BF16),
        compiler_params=_params(("arbitrary",), 40),
        name="norm_modulate",
    )(x, ctx, norm_w.reshape(1, d), mod)


NT_DIMS = (((1,), (1,)), ((), ()))


def _inproj_kernel(h_ref, wa_ref, wb_ref, b_ref, o_ref, *, n_plain):
    def emit(w):
        acc = lax.dot_general(h_ref[...], w.astype(BF16), NT_DIMS, preferred_element_type=F32)
        o_ref[...] = (acc + b_ref[...]).astype(o_ref.dtype)

    j = pl.program_id(1)

    @pl.when(j < n_plain)
    def _():
        emit(wa_ref[...])

    @pl.when(j >= n_plain)
    def _():
        emit(jnp.concatenate([wa_ref[DT_W:, :], wb_ref[...]], axis=0))


def in_proj(h, w_t, b_main, layer):
    m, k = h.shape
    tm = _pick(m, (2816, 1280, 1024, 768, 640, 512, 256, 128))
    tn = 256
    return pl.pallas_call(
        functools.partial(_inproj_kernel, n_plain=_SRC_DT // tn),
        grid=(m // tm, P_COLS // tn),
        in_specs=[pl.BlockSpec((tm, k), lambda i, j: (i, 0), pipeline_mode=pl.Buffered(1)),
                  pl.BlockSpec((None, tn, k), lambda i, j: (layer, j, 0)),
                  pl.BlockSpec((None, DT_W, k), lambda i, j: (layer, (j + 1) * (tn // DT_W), 0)),
                  pl.BlockSpec((1, tn), lambda i, j: (0, j))],
        out_specs=pl.BlockSpec((tm, tn), lambda i, j: (i, j)),
        out_shape=jax.ShapeDtypeStruct((m, P_COLS), BF16),
        compiler_params=_params(("arbitrary", "arbitrary"), 52),
        name="in_proj",
    )(h, w_t, w_t, b_main.reshape(1, P_COLS))


def _dtproj_kernel(h_ref, w_ref, b_ref, o_ref):
    acc = lax.dot_general(h_ref[...], w_ref[...].astype(BF16), NT_DIMS, preferred_element_type=F32)
    o_ref[...] = acc + b_ref[...]


def dt_proj(h, w_t, b_dt, layer):
    m, k = h.shape
    tm = _pick(m, (1408, 1280, 1024, 768, 640, 512, 256, 128))
    return pl.pallas_call(
        _dtproj_kernel,
        grid=(m // tm,),
        in_specs=[pl.BlockSpec((tm, k), lambda i: (i, 0)),
                  pl.BlockSpec((None, LANES, k), lambda i: (layer, _SRC_DT // LANES, 0)),
                  pl.BlockSpec((1, LANES), lambda i: (0, 0))],
        out_specs=pl.BlockSpec((tm, LANES), lambda i: (i, 0)),
        out_shape=jax.ShapeDtypeStruct((m, LANES), F32),
        compiler_params=_params(("arbitrary",), 48),
        name="dt_proj",
    )(h, w_t, b_dt.reshape(1, LANES))


def _conv_kernel(prev_ref, cur_ref, next_ref, w_ref, b_ref, o_ref, *, n_lat_chunks, n_chunks):
    c = pl.program_id(0)
    has_prev = jnp.logical_and(c != 0, c != n_lat_chunks)
    has_next = jnp.logical_and(c != n_lat_chunks - 1, c != n_chunks - 1)
    pad = CONV_K // 2
    cw = 512
    for jb in range(XBC_W // cw):
        sl = slice(jb * cw, (jb + 1) * cw)
        cur = cur_ref[:, sl].astype(F32)
        pv = prev_ref[SSM_CHUNK - 16:SSM_CHUNK, sl].astype(F32)[8:16]
        nx = next_ref[0:16, sl].astype(F32)[0:8]
        pv = jnp.where(has_prev, pv, 0.0)
        nx = jnp.where(has_next, nx, 0.0)
        ext = jnp.concatenate([pv, cur, nx], axis=0)
        acc = jnp.broadcast_to(b_ref[:, sl], (SSM_CHUNK, cw))
        for kk in range(CONV_K):
            lo = 8 + kk - pad
            acc = acc + w_ref[kk:kk + 1, sl] * ext[lo:lo + SSM_CHUNK]
        o_ref[:, sl] = _silu(acc).astype(o_ref.dtype)


def ssd_conv(p, conv_w, conv_b, n_lat_chunks):
    r = p.shape[0]
    n_chunks = r // SSM_CHUNK

    def prev_map(c):
        bad = jnp.logical_or(c == 0, c == n_lat_chunks)
        return (jnp.where(bad, c, c - 1), 0)

    def next_map(c):
        bad = jnp.logical_or(c == n_lat_chunks - 1, c == n_chunks - 1)
        return (jnp.where(bad, c, c + 1), 0)

    blk = (SSM_CHUNK, XBC_W)
    return pl.pallas_call(
        functools.partial(_conv_kernel, n_lat_chunks=n_lat_chunks, n_chunks=n_chunks),
        grid=(n_chunks,),
        in_specs=[pl.BlockSpec(blk, prev_map),
                  pl.BlockSpec(blk, lambda c: (c, 0)),
                  pl.BlockSpec(blk, next_map),
                  pl.BlockSpec((8, XBC_W), lambda c: (0, 0)),
                  pl.BlockSpec((1, XBC_W), lambda c: (0, 0))],
        out_specs=pl.BlockSpec(blk, lambda c: (c, 0)),
        out_shape=jax.ShapeDtypeStruct((r, XBC_W), BF16),
        compiler_params=_params(("arbitrary",), 32),
        name="ssd_conv",
    )(p, p, p, jnp.pad(conv_w, ((0, 8 - CONV_K), (0, 0))), conv_b.reshape(1, XBC_W))


def _dot3(tri, v):
    hi = v.astype(BF16)
    r1 = v - hi.astype(F32)
    mid = r1.astype(BF16)
    lo = (r1 - mid.astype(F32)).astype(BF16)
    return (jnp.dot(tri, hi, preferred_element_type=F32)
            + jnp.dot(tri, mid, preferred_element_type=F32)
            + jnp.dot(tri, lo, preferred_element_type=F32))


def _pair_cols(v, ha, lane_lo):
    rows = v.shape[0]
    a = jnp.broadcast_to(v[:, ha:ha + 1], (rows, LANES))
    b = jnp.broadcast_to(v[:, ha + 1:ha + 2], (rows, LANES))
    return jnp.where(lane_lo, a, b)


SSD_STEP_CHUNKS = 1


def _ssd_chunk(u_ref, dt_ref, cst_ref, spread_ref, ht_ref, emit, rs, *, reverse):
    L = SSM_CHUNK
    hoff = SSM_HEADS if reverse else 0
    li = lax.broadcasted_iota(jnp.int32, (L, L), 0)
    si = lax.broadcasted_iota(jnp.int32, (L, L), 1)
    keep = (li <= si) if reverse else (li >= si)
    tri = jnp.where((si >= li) if reverse else (si <= li), 1.0, 0.0).astype(BF16)
    lane_lo = lax.broadcasted_iota(jnp.int32, (L, LANES), 1) < SSM_HEAD_DIM
    lane_lo1 = lax.broadcasted_iota(jnp.int32, (1, LANES), 1) < SSM_HEAD_DIM

    raw = dt_ref[rs, :] + cst_ref[0:1, :]
    dtv = jnp.maximum(raw, 0.0) + jnp.log1p(jnp.exp(-jnp.abs(raw)))
    da = dtv * cst_ref[1:2, :]
    acs = _dot3(tri, da)
    acs_t = acs.T
    dt_t = dtv.T
    a_last = acs[0:1, :] if reverse else acs[L - 1:L, :]
    e_last = jnp.exp(a_last)
    per_head = jnp.concatenate([jnp.exp(a_last - acs) * dtv, jnp.exp(acs)], axis=0).astype(BF16)

    pairs = GROUP_W // LANES
    for g in range(SSM_GROUPS):
        bg = u_ref[rs, BRANCH_W + g * SSM_STATE:BRANCH_W + (g + 1) * SSM_STATE]
        cg = u_ref[rs, BRANCH_W + SSM_GN + g * SSM_STATE:BRANCH_W + SSM_GN + (g + 1) * SSM_STATE]
        cb = lax.dot_general(cg, bg, NT_DIMS, preferred_element_type=F32)
        bg_t = bg.astype(F32).T.astype(BF16)
        spread = jnp.dot(per_head, spread_ref[g], preferred_element_type=F32)
        y_parts, d_parts = [], []
        for pi in range(pairs):
            ha = hoff + g * (2 * pairs) + 2 * pi
            xp = u_ref[rs, g * GROUP_W + pi * LANES:g * GROUP_W + (pi + 1) * LANES]
            zero = jnp.zeros_like(xp)
            y_pair = None
            for hh, xm in ((ha, jnp.where(lane_lo, xp, zero)), (ha + 1, jnp.where(lane_lo, zero, xp))):
                diff = acs[:, hh:hh + 1] - acs_t[hh:hh + 1, :]
                dec = jnp.exp(jnp.where(keep, diff, -jnp.inf))
                m = (cb * dec * dt_t[hh:hh + 1, :]).astype(BF16)
                yh = jnp.dot(m, xm, preferred_element_type=F32)
                y_pair = yh if y_pair is None else y_pair + yh
            y_parts.append(y_pair)
            d_parts.append(_pair_cols(e_last, ha, lane_lo1))
        xw = (u_ref[rs, g * GROUP_W:(g + 1) * GROUP_W].astype(F32) * spread[:L]).astype(BF16)
        hs = ht_ref[g]
        y_off = jnp.dot(cg, hs.astype(BF16), preferred_element_type=F32) * spread[L:]
        s_new = jnp.dot(bg_t, xw, preferred_element_type=F32)
        ht_ref[g] = hs * jnp.concatenate(d_parts, axis=1) + s_new
        emit(g, jnp.concatenate(y_parts, axis=1) + y_off)


def _ssd_fwd_kernel(u_ref, dt_ref, cst_ref, spread_ref, o_ref, ht_ref):
    @pl.when(pl.program_id(0) == 0)
    def _():
        ht_ref[...] = jnp.zeros(ht_ref.shape, F32)

    for ci in range(SSD_STEP_CHUNKS):
        rs = slice(ci * SSM_CHUNK, (ci + 1) * SSM_CHUNK)

        def emit(g, y, rs=rs):
            o_ref[rs, g * GROUP_W:(g + 1) * GROUP_W] = y

        _ssd_chunk(u_ref, dt_ref, cst_ref, spread_ref, ht_ref, emit, rs, reverse=False)


def _ssd_bwd_kernel(u_ref, dt_ref, cst_ref, spread_ref, yf_ref, z0_ref, z1_ref, dsk_ref, nw_ref, o_ref, ht_ref, y_ref):
    @pl.when(pl.program_id(0) == 0)
    def _():
        ht_ref[...] = jnp.zeros(ht_ref.shape, F32)

    half = SSM_GROUPS // 2

    for ci in reversed(range(SSD_STEP_CHUNKS)):
        rs = slice(ci * SSM_CHUNK, (ci + 1) * SSM_CHUNK)

        def emit(g, y, rs=rs):
            sl = slice(g * GROUP_W, (g + 1) * GROUP_W)
            z_ref = z0_ref if g < half else z1_ref
            zsl = slice((g % half) * GROUP_W, (g % half + 1) * GROUP_W)
            tot = y + yf_ref[rs, sl] + dsk_ref[:, sl] * u_ref[rs, sl].astype(F32)
            y_ref[rs, sl] = tot * _silu(z_ref[rs, zsl].astype(F32))

        _ssd_chunk(u_ref, dt_ref, cst_ref, spread_ref, ht_ref, emit, rs, reverse=True)
    yv = y_ref[...]
    o_ref[...] = (yv * lax.rsqrt(jnp.mean(yv * yv, axis=-1, keepdims=True) + EPS) * nw_ref[...]).astype(o_ref.dtype)


def ssd_branch(u, dt_raw, cst, p, d_exp, ssm_norm_w, n_ctx_rows):
    r = u.shape[0]
    step_rows = SSD_STEP_CHUNKS * SSM_CHUNK
    n_chunks = r // step_rows
    n_ctx_chunks = n_ctx_rows // step_rows
    n_lat = n_chunks - n_ctx_chunks
    state = pltpu.VMEM((SSM_GROUPS, SSM_STATE, GROUP_W), F32)

    def fwd_idx(i):
        return jnp.where(i < n_ctx_chunks, n_lat + i, i - n_ctx_chunks)

    def bwd_idx(i):
        return n_chunks - 1 - i

    heads_per_group = SSM_HEADS // SSM_GROUPS
    hh = np.arange(LANES)[None, None, :, None]
    owner = (np.arange(2)[:, None, None, None] * SSM_HEADS + np.arange(SSM_GROUPS)[None, :, None, None] * heads_per_group
             + np.arange(GROUP_W)[None, None, None, :] // SSM_HEAD_DIM)
    spread = jnp.asarray(hh == owner, BF16)

    def spread_spec(direction):
        return pl.BlockSpec((None, SSM_GROUPS, LANES, GROUP_W), lambda i: (direction, 0, 0, 0))

    ublk = (step_rows, XBC_W)
    yf = pl.pallas_call(
        _ssd_fwd_kernel,
        grid=(n_chunks,),
        in_specs=[pl.BlockSpec(ublk, lambda i: (fwd_idx(i), 0)),
                  pl.BlockSpec((step_rows, LANES), lambda i: (fwd_idx(i), 0)),
                  pl.BlockSpec((8, LANES), lambda i: (0, 0)),
                  spread_spec(0)],
        out_specs=pl.BlockSpec((step_rows, BRANCH_W), lambda i: (fwd_idx(i), 0)),
        out_shape=jax.ShapeDtypeStruct((r, BRANCH_W), F32),
        scratch_shapes=[state],
        compiler_params=_params(("arbitrary",), 32),
        name="ssd_fwd",
    )(u, dt_raw, cst, spread)

    zw = BRANCH_W // 2
    z0 = OFF_Z // zw
    return pl.pallas_call(
        _ssd_bwd_kernel,
        grid=(n_chunks,),
        in_specs=[pl.BlockSpec(ublk, lambda i: (bwd_idx(i), 0)),
                  pl.BlockSpec((step_rows, LANES), lambda i: (bwd_idx(i), 0)),
                  pl.BlockSpec((8, LANES), lambda i: (0, 0)),
                  spread_spec(1),
                  pl.BlockSpec((step_rows, BRANCH_W), lambda i: (bwd_idx(i), 0)),
                  pl.BlockSpec((step_rows, zw), lambda i: (bwd_idx(i), z0)),
                  pl.BlockSpec((step_rows, zw), lambda i: (bwd_idx(i), z0 + 1)),
                  pl.BlockSpec((1, BRANCH_W), lambda i: (0, 0)),
                  pl.BlockSpec((1, BRANCH_W), lambda i: (0, 0))],
        out_specs=pl.BlockSpec((step_rows, BRANCH_W), lambda i: (bwd_idx(i), 0)),
        out_shape=jax.ShapeDtypeStruct((r, BRANCH_W), BF16),
        scratch_shapes=[state, pltpu.VMEM((step_rows, BRANCH_W), F32)],
        compiler_params=_params(("arbitrary",), 32),
        name="ssd_bwd",
    )(u, dt_raw, cst, spread, yf, p, p, d_exp, ssm_norm_w.reshape(1, BRANCH_W))


def _qkprep_kernel(qk_ref, cos_ref, sin_ref, qw_ref, kw_ref, q_ref, k_ref):
    cosv = cos_ref[...]
    sinv = sin_ref[...]
    scale = HEAD_DIM ** -0.5 * math.log2(math.e)
    even = lax.broadcasted_iota(jnp.int32, cosv.shape, 1) % 2 == 0
    for h in range(ATT_HEADS + KV_HEADS):
        t = qk_ref[:, h * HEAD_DIM:(h + 1) * HEAD_DIM].astype(F32)
        w = qw_ref[...] if h < ATT_HEADS else kw_ref[...]
        n = t * lax.rsqrt(jnp.mean(t * t, axis=-1, keepdims=True) + EPS) * w
        partner = jnp.where(even, pltpu.roll(n, HEAD_DIM - 1, 1), pltpu.roll(n, 1, 1))
        rot = n * cosv + partner * sinv
        if h < ATT_HEADS:
            q_ref[:, h * HEAD_DIM:(h + 1) * HEAD_DIM] = (rot * scale).astype(q_ref.dtype)
        else:
            hk = h - ATT_HEADS
            k_ref[:, hk * HEAD_DIM:(hk + 1) * HEAD_DIM] = rot.astype(k_ref.dtype)


def qk_prep(p, cos_t, sin_t, qw, kw):
    r = p.shape[0]
    tr = 256
    kvw = KV_HEADS * HEAD_DIM
    return pl.pallas_call(
        _qkprep_kernel,
        grid=(r // tr,),
        in_specs=[pl.BlockSpec((tr, QK_W), lambda i: (i, OFF_QA // QK_W)),
                  pl.BlockSpec((tr, HEAD_DIM), lambda i: (i, 0)),
                  pl.BlockSpec((tr, HEAD_DIM), lambda i: (i, 0)),
                  pl.BlockSpec((1, HEAD_DIM), lambda i: (0, 0)),
                  pl.BlockSpec((1, HEAD_DIM), lambda i: (0, 0))],
        out_specs=[pl.BlockSpec((tr, BRANCH_W), lambda i: (i, 0)),
                   pl.BlockSpec((tr, kvw), lambda i: (i, 0))],
        out_shape=[jax.ShapeDtypeStruct((r, BRANCH_W), BF16),
                   jax.ShapeDtypeStruct((r, kvw), BF16)],
        compiler_params=_params(("arbitrary",), 32),
        name="qk_prep",
    )(p, cos_t, sin_t, qw, kw)


def _flash_kernel(q_ref, k_ref, v_ref, g_ref, o_ref, m_ref, l_ref, acc_ref, *, nk):
    ki = pl.program_id(2)

    @pl.when(ki == 0)
    def _():
        m_ref[...] = jnp.full(m_ref.shape, -jnp.inf, F32)
        l_ref[...] = jnp.zeros(l_ref.shape, F32)
        acc_ref[...] = jnp.zeros(acc_ref.shape, F32)

    k = k_ref[...]
    v1 = jnp.concatenate([v_ref[...], jnp.ones(v_ref.shape, v_ref.dtype)], axis=1)
    reps = k.shape[0] // LANES
    for r in range(ATT_GROUP):
        hs = slice(r * HEAD_DIM, (r + 1) * HEAD_DIM)
        s = lax.dot_general(q_ref[:, hs], k, NT_DIMS, preferred_element_type=F32)
        m_prev = m_ref[r]
        m_next = jnp.maximum(m_prev, jnp.max(s, axis=1, keepdims=True))
        p = jnp.exp2(s - jnp.concatenate([m_next] * reps, axis=1))
        alpha = jnp.exp2(m_prev - m_next)
        pv = jnp.dot(p.astype(v1.dtype), v1, preferred_element_type=F32)
        l_ref[r] = alpha * l_ref[r] + pv[:, HEAD_DIM:]
        acc_ref[r] = alpha * acc_ref[r] + pv[:, :HEAD_DIM]
        m_ref[r] = m_next

    @pl.when(ki == nk - 1)
    def _():
        for r in range(ATT_GROUP):
            hs = slice(r * HEAD_DIM, (r + 1) * HEAD_DIM)
            o = acc_ref[r] / l_ref[r]
            o_ref[:, hs] = (_silu(g_ref[:, hs].astype(F32)) * o).astype(o_ref.dtype)


def flash_gqa(qn, kn, p, *, q_rows, q_row0, k_rows, k_row0, tq, tk):
    nq = q_rows // tq
    nk = k_rows // tk
    qb0 = q_row0 // tq
    kb0 = k_row0 // tk
    gw = ATT_GROUP * HEAD_DIM
    return pl.pallas_call(
        functools.partial(_flash_kernel, nk=nk),
        grid=(KV_HEADS, nq, nk),
        in_specs=[pl.BlockSpec((tq, gw), lambda g, i, j: (qb0 + i, g)),
                  pl.BlockSpec((tk, HEAD_DIM), lambda g, i, j: (kb0 + j, g)),
                  pl.BlockSpec((tk, HEAD_DIM), lambda g, i, j: (kb0 + j, OFF_VA // HEAD_DIM + g)),
                  pl.BlockSpec((tq, gw), lambda g, i, j: (qb0 + i, OFF_GA // gw + g))],
        out_specs=pl.BlockSpec((tq, gw), lambda g, i, j: (i, g)),
        out_shape=jax.ShapeDtypeStruct((q_rows, BRANCH_W), BF16),
        scratch_shapes=[pltpu.VMEM((ATT_GROUP, tq, LANES), F32),
                        pltpu.VMEM((ATT_GROUP, tq, LANES), F32),
                        pltpu.VMEM((ATT_GROUP, tq, HEAD_DIM), F32)],
        compiler_params=_params(("arbitrary", "arbitrary", "arbitrary"), 48),
        name="flash_gqa",
    )(qn, kn, p, p)


NA_QROWS = 4
NA_KROWS = 12
NA_PATTERNS = 3


def _na_rel_row(pattern, qr, kr):
    if pattern == 0:
        return kr - qr + (NA_WIN_R - 1) if kr < NA_WIN_R else None
    if pattern == 1:
        return kr - qr + (NA_WIN_R - 1 - NA_WIN_R // 2) if 0 <= kr - qr < NA_WIN_R else None
    first = NA_KROWS - NA_WIN_R
    return kr - qr + (NA_WIN_R - 1) - (NA_KROWS - NA_QROWS) if kr >= first else None


def _na_kernel(q_ref, k_ref, v_ref, g_ref, tp_ref, o_ref, bias_ref, qs_ref, v1_ref, *, n_lat, n_ctx, grid_rows,
               prep_rows):
    qn = NA_QROWS * GRID_W
    win = NA_KROWS * GRID_W
    n_blocks = grid_rows // NA_QROWS
    rows = n_lat + n_ctx
    for r0 in range(0, rows, prep_rows):
        rs = slice(r0, r0 + prep_rows)
        qs_ref[rs, :] = (q_ref[rs, :].astype(F32) * (HEAD_DIM ** -0.5 * LOG2E)).astype(BF16)
        v1_ref[rs, 0:HEAD_DIM] = v_ref[rs, :]
        v1_ref[rs, HEAD_DIM:2 * HEAD_DIM] = jnp.ones((prep_rows, HEAD_DIM), BF16)
    kc = k_ref[n_lat:rows, :]
    vc = v1_ref[n_lat:rows, :]

    lane_lo = lax.broadcasted_iota(jnp.int32, (GRID_W, LANES), 1) < GRID_W
    for pattern in range(NA_PATTERNS):
        for qr in range(NA_QROWS):
            for kp in range(NA_KROWS // 2):
                ja = _na_rel_row(pattern, qr, 2 * kp)
                jb = _na_rel_row(pattern, qr, 2 * kp + 1)
                if ja is None and jb is None:
                    tile = jnp.full((GRID_W, LANES), NEG_BIAS, F32)
                elif jb is None:
                    tile = jnp.where(lane_lo, tp_ref[ja + 1], NEG_BIAS)
                elif ja is None:
                    tile = jnp.where(lane_lo, NEG_BIAS, tp_ref[jb])
                else:
                    tile = tp_ref[jb]
                bias_ref[pattern, qr * GRID_W:(qr + 1) * GRID_W, kp * LANES:(kp + 1) * LANES] = tile

    def block_body(b, carry):
        u0 = jnp.clip(b * NA_QROWS - NA_WIN_R // 2, 0, grid_rows - NA_KROWS)
        pattern = jnp.where(b == 0, 0, jnp.where(b == n_blocks - 1, 2, 1))
        qs = pl.ds(pl.multiple_of(b * qn, qn), qn)
        ws = pl.ds(pl.multiple_of(u0 * GRID_W, GRID_W), win)
        q = qs_ref[qs, :]
        s_loc = lax.dot_general(q, k_ref[ws, :], NT_DIMS, preferred_element_type=F32) + bias_ref[pattern]
        s_ctx = lax.dot_general(q, kc, NT_DIMS, preferred_element_type=F32)
        m = jnp.maximum(jnp.max(s_loc, axis=1, keepdims=True), jnp.max(s_ctx, axis=1, keepdims=True))
        pv = (jnp.dot(jnp.exp2(s_loc - m).astype(BF16), v1_ref[ws, :], preferred_element_type=F32)
              + jnp.dot(jnp.exp2(s_ctx - m).astype(BF16), vc, preferred_element_type=F32))
        o = pv[:, :HEAD_DIM] / pv[:, HEAD_DIM:]
        o_ref[qs, :] = (_silu(g_ref[qs, :].astype(F32)) * o).astype(o_ref.dtype)
        return carry

    lax.fori_loop(0, n_blocks, block_body, 0, unroll=8)

    qs = slice(n_lat, rows)
    s = lax.dot_general(qs_ref[qs, :], kc, NT_DIMS, preferred_element_type=F32)
    pvc = jnp.dot(jnp.exp2(s - jnp.max(s, axis=1, keepdims=True)).astype(BF16), vc, preferred_element_type=F32)
    oc = pvc[:, :HEAD_DIM] / pvc[:, HEAD_DIM:]
    o_ref[qs, :] = (_silu(g_ref[qs, :].astype(F32)) * oc).astype(o_ref.dtype)


def neighbourhood(p, tp, n_lat):
    r = p.shape[0]
    n_ctx = r - n_lat
    col = lambda off: (lambda h: (0, off // HEAD_DIM + h))
    return pl.pallas_call(
        functools.partial(_na_kernel, n_lat=n_lat, n_ctx=n_ctx, grid_rows=n_lat // GRID_W,
                          prep_rows=_pick(r, (1056, 640, 256, 128))),
        grid=(NA_HEADS,),
        in_specs=[pl.BlockSpec((r, HEAD_DIM), col(OFF_QN)),
                  pl.BlockSpec((r, HEAD_DIM), col(OFF_KN)),
                  pl.BlockSpec((r, HEAD_DIM), col(OFF_VN)),
                  pl.BlockSpec((r, HEAD_DIM), col(OFF_GN)),
                  pl.BlockSpec((None, 2 * NA_WIN_R, GRID_W, 2 * GRID_W), lambda h: (h, 0, 0, 0))],
        out_specs=pl.BlockSpec((r, HEAD_DIM), lambda h: (0, h)),
        out_shape=jax.ShapeDtypeStruct((r, BRANCH_W), BF16),
        scratch_shapes=[pltpu.VMEM((NA_PATTERNS, NA_QROWS * GRID_W, NA_KROWS * GRID_W), F32),
                        pltpu.VMEM((r, HEAD_DIM), BF16),
                        pltpu.VMEM((r, 2 * HEAD_DIM), BF16)],
        compiler_params=_params(("arbitrary",), 48),
        name="neighbourhood",
    )(p, p, p, p, tp)


def na_bias_table(rpb):
    n_rel = 2 * NA_WIN_C - 1
    c = np.arange(GRID_W)[:, None]
    kc = np.arange(GRID_W)[None, :]
    c0 = np.clip(c - NA_WIN_C // 2, 0, GRID_W - NA_WIN_C)
    valid = (kc >= c0) & (kc < c0 + NA_WIN_C)
    rel = kc - c + (NA_WIN_C - 1)
    onehot = (rel[None] == np.arange(n_rel)[:, None, None]) & valid[None]
    h, nr, _ = rpb.shape
    t = jnp.dot(rpb.astype(F32).reshape(h * nr, n_rel), jnp.asarray(onehot.reshape(n_rel, -1), F32),
                precision=lax.Precision.HIGHEST).reshape(h, nr, GRID_W, GRID_W)
    t = jnp.where(jnp.asarray(valid), t * LOG2E, NEG_BIAS)
    masked = jnp.full((h, 1, GRID_W, GRID_W), NEG_BIAS, F32)
    tpad = jnp.concatenate([masked, t, masked], axis=1)
    return jnp.concatenate([tpad[:, :-1], tpad[:, 1:]], axis=-1)


def _gateup_kernel(a_ref, b_ref, c_ref, wa_ref, wb_ref, wc_ref, ga_ref, gb_ref, gc_ref, o_ref):
    acc = None
    for br_ref, w_ref, g_ref in ((a_ref, wa_ref, ga_ref), (b_ref, wb_ref, gb_ref), (c_ref, wc_ref, gc_ref)):
        up = jnp.dot(br_ref[...], w_ref[...], preferred_element_type=F32)
        term = _sigmoid(g_ref[...].astype(F32)) * up
        acc = term if acc is None else acc + term
    o_ref[...] = acc.astype(o_ref.dtype)


def gate_up(br_a, br_b, br_c, w_up, p, layer, *, rows, row0s, p_row0):
    tm = _pick(rows, (1024, 512, 256))
    tn = 512

    def br_spec(row0):
        return pl.BlockSpec((tm, BRANCH_W), lambda i, j: (row0 // tm + i, 0))

    def w_spec(b):
        return pl.BlockSpec((None, None, BRANCH_W, tn), lambda i, j: (layer, b, 0, j))

    def g_spec(b):
        return pl.BlockSpec((tm, tn), lambda i, j: (p_row0 // tm + i, (OFF_GM + b * D_MODEL) // tn + j))

    return pl.pallas_call(
        _gateup_kernel,
        grid=(rows // tm, D_MODEL // tn),
        in_specs=[br_spec(row0s[0]), br_spec(row0s[1]), br_spec(row0s[2]),
                  w_spec(0), w_spec(1), w_spec(2), g_spec(0), g_spec(1), g_spec(2)],
        out_specs=pl.BlockSpec((tm, tn), lambda i, j: (i, j)),
        out_shape=jax.ShapeDtypeStruct((rows, D_MODEL), BF16),
        compiler_params=_params(("arbitrary", "arbitrary"), 56),
        name="gate_up",
    )(br_a, br_b, br_c, w_up, w_up, w_up, p, p, p)


def _outproj_kernel(s_ref, w_ref, x_ref, mod_ref, o_ref, *, mod_row):
    y = jnp.dot(s_ref[...], w_ref[...], preferred_element_type=F32)
    o_ref[...] = x_ref[...] + mod_ref[mod_row:mod_row + 1, :] * y


def out_proj(s, w_out, x, mod, layer, mod_row):
    rows = s.shape[0]
    tm = _pick(rows, (1024, 512, 256))
    tn = 512
    return pl.pallas_call(
        functools.partial(_outproj_kernel, mod_row=mod_row),
        grid=(rows // tm, D_MODEL // tn),
        in_specs=[pl.BlockSpec((tm, D_MODEL), lambda i, j: (i, 0)),
                  pl.BlockSpec((None, D_MODEL, tn), lambda i, j: (layer, 0, j)),
                  pl.BlockSpec((tm, tn), lambda i, j: (i, j)),
                  pl.BlockSpec((8, tn), lambda i, j: (0, 2 * D_MODEL // tn + j))],
        out_specs=pl.BlockSpec((tm, tn), lambda i, j: (i, j)),
        out_shape=jax.ShapeDtypeStruct((rows, D_MODEL), F32),
        compiler_params=_params(("arbitrary", "arbitrary"), 48),
        name="out_proj",
    )(s, w_out, x, mod)


def _final_norm_kernel(x_ref, w_ref, o_ref):
    xv = x_ref[...]
    o_ref[...] = xv * lax.rsqrt(jnp.mean(xv * xv, axis=-1, keepdims=True) + EPS) * w_ref[...]


def final_norm(x, w):
    t, d = x.shape
    tr = 256
    return pl.pallas_call(
        _final_norm_kernel,
        grid=(t // tr,),
        in_specs=[pl.BlockSpec((tr, d), lambda i: (i, 0)), pl.BlockSpec((1, d), lambda i: (0, 0))],
        out_specs=pl.BlockSpec((tr, d), lambda i: (i, 0)),
        out_shape=jax.ShapeDtypeStruct((t, d), F32),
        compiler_params=_params(("arbitrary",), 40),
        name="final_norm",
    )(x, w.reshape(1, d))


def _rope_tables(n_lat, n_ctx):
    pos = jnp.arange(n_lat, dtype=jnp.int32)
    row = (pos // GRID_W).astype(F32)
    colp = (pos % GRID_W).astype(F32)
    n_freq = HEAD_DIM // 4
    inv_freq = ROPE_THETA ** (-jnp.arange(n_freq, dtype=F32) / n_freq)
    ang = jnp.concatenate([row[:, None] * inv_freq, colp[:, None] * inv_freq], axis=-1)
    cos, sin = jnp.cos(ang), jnp.sin(ang)
    cos_t = jnp.repeat(cos, 2, axis=-1)
    sin_t = jnp.stack([-sin, sin], axis=-1).reshape(n_lat, HEAD_DIM)
    cos_t = jnp.concatenate([cos_t, jnp.ones((n_ctx, HEAD_DIM), F32)], axis=0)
    sin_t = jnp.concatenate([sin_t, jnp.zeros((n_ctx, HEAD_DIM), F32)], axis=0)
    return cos_t, sin_t


def hybrid_layer(x, ctx, sb, lp, cos_t, sin_t, update_ctx):
    t = x.shape[0]
    lc = ctx.shape[0]
    layer = lp["layer"]
    mod = modulation(sb, lp["w_mod"], lp["b_mod"], layer)
    h = norm_modulate(x, ctx, lp["norm_w"], mod)
    p = in_proj(h, lp["w_in_t"], lp["b_main"], layer)
    dt_raw = dt_proj(h, lp["w_in_t"], lp["b_dt"], layer)

    u = ssd_conv(p, lp["conv_w"], lp["conv_b"], t // SSM_CHUNK)
    br_a = ssd_branch(u, dt_raw, lp["ssd_cst"], p, lp["d_exp"], lp["ssm_norm_w"], lc)

    qn, kn = qk_prep(p, cos_t, sin_t, lp["q_norm_w"], lp["k_norm_w"])
    tq = _pick(t, (1024, 512, 256))
    tk = _pick(t + lc, (1408, 768, 640, 512, 384, 256))
    br_b = flash_gqa(qn, kn, p, q_rows=t, q_row0=0, k_rows=t + lc, k_row0=0, tq=tq, tk=tk)

    br_c = neighbourhood(p, lp["na_bias"], t)

    s = gate_up(br_a, br_b, br_c, lp["w_up"], p, layer, rows=t, row0s=(0, 0, 0), p_row0=0)
    x_new = out_proj(s, lp["w_out"], x, mod, layer, 0)
    if not update_ctx:
        return x_new, ctx
    br_b_c = flash_gqa(qn, kn, p, q_rows=lc, q_row0=t, k_rows=lc, k_row0=t, tq=lc, tk=lc)
    s_c = gate_up(br_a, br_b_c, br_c, lp["w_up"], p, layer, rows=lc, row0s=(t, 0, t), p_row0=t)
    ctx_new = out_proj(s_c, lp["w_out"], ctx, mod, layer, 1)
    return x_new, ctx_new


def layer_params(layer, w_mod, b_mod, norm_w, w_in_t, b_in, conv_w, conv_b, a_log, dt_bias, d_skip, ssm_norm_w,
                 q_norm_w, k_norm_w, rpb, w_up, w_out):
    pad_heads = [(0, 0), (0, LANES - DT_W)]
    b = b_in[layer].astype(F32)
    cst = jnp.concatenate([
        jnp.pad(dt_bias[layer].astype(F32).reshape(1, DT_W), pad_heads),
        jnp.pad(-jnp.exp(a_log[layer].astype(F32)).reshape(1, DT_W), pad_heads),
        jnp.zeros((6, LANES), F32)], axis=0)
    return dict(
        layer=layer, w_mod=w_mod, b_mod=b_mod[layer], norm_w=norm_w[layer], w_in_t=w_in_t,
        b_main=jnp.concatenate([b[:_SRC_DT], b[_SRC_QA:]]),
        b_dt=jnp.pad(b[_SRC_DT:_SRC_QA], (0, LANES - DT_W)),
        conv_w=conv_w[layer], conv_b=conv_b[layer], ssd_cst=cst,
        d_exp=jnp.repeat(d_skip[layer].astype(F32), SSM_HEAD_DIM).reshape(1, BRANCH_W),
        ssm_norm_w=ssm_norm_w[layer],
        q_norm_w=q_norm_w[layer].reshape(1, HEAD_DIM), k_norm_w=k_norm_w[layer].reshape(1, HEAD_DIM),
        na_bias=na_bias_table(rpb[layer]), w_up=w_up, w_out=w_out)


def kernel(x, c, ctx, c_ctx, norm_w, w_mod, b_mod, w_in, b_in, conv_w, conv_b, a_log, dt_bias, d_skip, ssm_norm_w, q_norm_w, k_norm_w, rpb, w_up, w_out, final_norm_w):
    xs = x[0]
    cs = ctx[0]
    sb = jnp.broadcast_to(jnp.stack([c[0], c_ctx])[:, :, None], (2, D_MODEL, LANES))
    cos_t, sin_t = _rope_tables(xs.shape[0], cs.shape[0])
    w_in_t = jnp.swapaxes(w_in, 1, 2)
    w_up_b = w_up.astype(BF16)
    w_out_b = w_out.astype(BF16)
    for layer in range(DEPTH):
        lp = layer_params(layer, w_mod, b_mod, norm_w, w_in_t, b_in, conv_w, conv_b, a_log, dt_bias, d_skip,
                          ssm_norm_w, q_norm_w, k_norm_w, rpb, w_up_b, w_out_b)
        xs, cs = hybrid_layer(xs, cs, sb, lp, cos_t, sin_t, layer < DEPTH - 1)
    return final_norm(xs, final_norm_w)[None]
```

```python
import functools
import math

import numpy as np
import jax
import jax.numpy as jnp
from jax import lax
from jax.experimental import pallas as pl
from jax.experimental.pallas import tpu as pltpu

F32 = jnp.float32
BF16 = jnp.bfloat16

D_MODEL = 4096
DEPTH = 2
GRID_W = 64
N_BRANCH = 3
BRANCH_W = 2048
EPS = 1e-6

SSM_HEADS = 32
SSM_HEAD_DIM = 64
SSM_GROUPS = 4
SSM_STATE = 128
SSM_CHUNK = 128
CONV_K = 5
SSM_GN = SSM_GROUPS * SSM_STATE
XBC_W = BRANCH_W + 2 * SSM_GN
DT_W = 2 * SSM_HEADS
GROUP_W = (SSM_HEADS // SSM_GROUPS) * SSM_HEAD_DIM

ATT_HEADS = 16
KV_HEADS = 4
ATT_GROUP = ATT_HEADS // KV_HEADS
HEAD_DIM = 128
ROPE_THETA = 10000.0

NA_HEADS = 16
NA_WIN_R = 8
NA_WIN_C = 16

LANES = 128
MIB = 1024 * 1024

_SRC_DT = XBC_W + BRANCH_W
_SRC_QA = _SRC_DT + 2 * SSM_HEADS
_SRC_VA = _SRC_QA + BRANCH_W + KV_HEADS * HEAD_DIM
OFF_XBC = 0
OFF_Z = OFF_XBC + XBC_W
OFF_QA = OFF_Z + BRANCH_W
OFF_KA = OFF_QA + BRANCH_W
OFF_VA = OFF_KA + KV_HEADS * HEAD_DIM
OFF_GA = OFF_VA + KV_HEADS * HEAD_DIM
OFF_QN = OFF_GA + BRANCH_W
OFF_KN = OFF_QN + BRANCH_W
OFF_VN = OFF_KN + BRANCH_W
OFF_GN = OFF_VN + BRANCH_W
OFF_GM = OFF_GN + BRANCH_W
P_COLS = OFF_GM + N_BRANCH * D_MODEL
QK_W = BRANCH_W + KV_HEADS * HEAD_DIM

NEG_BIAS = -1e30
LOG2E = math.log2(math.e)


def _params(sem, vmem_mib):
    return pltpu.CompilerParams(dimension_semantics=sem, vmem_limit_bytes=vmem_mib * MIB)


def _pick(n, candidates):
    for c in candidates:
        if n % c == 0:
            return c
    raise ValueError(f"no tile for {n} in {candidates}")


def _sigmoid(v):
    return 1.0 / (1.0 + jnp.exp(-v))


def _silu(v):
    return v * _sigmoid(v)


def _mod_kernel(s_ref, w_ref, b_ref, o_ref, act_ref):
    tn = w_ref.shape[1]
    k = w_ref.shape[0]

    @pl.when(pl.program_id(0) == 0)
    def _():
        act_ref[...] = _silu(s_ref[...])

    o_ref[...] = jnp.zeros(o_ref.shape, F32)
    for v in range(2):
        s = act_ref[v]
        for jb in range(tn // LANES):
            sl = slice(jb * LANES, (jb + 1) * LANES)
            prod = w_ref[:, sl] * s
            part = prod.reshape(k // 8, 8, LANES).sum(axis=0)
            o_ref[v:v + 1, sl] = part.sum(axis=0, keepdims=True) + b_ref[:, sl]


def modulation(sb, w_mod, b_mod, layer):
    _, k, n = w_mod.shape
    tn = 512
    return pl.pallas_call(
        _mod_kernel,
        grid=(n // tn,),
        in_specs=[pl.BlockSpec((2, k, LANES), lambda j: (0, 0, 0)),
                  pl.BlockSpec((None, k, tn), lambda j: (layer, 0, j)),
                  pl.BlockSpec((1, tn), lambda j: (0, j))],
        out_specs=pl.BlockSpec((8, tn), lambda j: (0, j)),
        out_shape=jax.ShapeDtypeStruct((8, n), F32),
        scratch_shapes=[pltpu.VMEM((2, k, LANES), F32)],
        compiler_params=_params(("arbitrary",), 40),
        name="modulation",
    )(sb, w_mod, b_mod.reshape(1, n))


def _norm_kernel(x_ref, c_ref, nw_ref, mod_ref, o_ref, *, n_lat_blocks):
    i = pl.program_id(0)

    def emit(src_ref, row):
        xv = src_ref[...]
        y = xv * lax.rsqrt(jnp.mean(xv * xv, axis=-1, keepdims=True) + EPS) * nw_ref[...]
        shift = mod_ref[row:row + 1, 0:D_MODEL]
        scale = mod_ref[row:row + 1, D_MODEL:2 * D_MODEL]
        o_ref[...] = (y * (1.0 + scale) + shift).astype(o_ref.dtype)

    @pl.when(i < n_lat_blocks)
    def _():
        emit(x_ref, 0)

    @pl.when(i >= n_lat_blocks)
    def _():
        emit(c_ref, 1)


def norm_modulate(x, ctx, norm_w, mod):
    t, d = x.shape
    lc = ctx.shape[0]
    tr = 256
    n_lat = t // tr
    n_ctx = lc // tr
    return pl.pallas_call(
        functools.partial(_norm_kernel, n_lat_blocks=n_lat),
        grid=(n_lat + n_ctx,),
        in_specs=[pl.BlockSpec((tr, d), lambda i: (jnp.minimum(i, n_lat - 1), 0)),
                  pl.BlockSpec((tr, d), lambda i: (jnp.maximum(i - n_lat, 0), 0)),
                  pl.BlockSpec((1, d), lambda i: (0, 0)),
                  pl.BlockSpec((8, 3 * d), lambda i: (0, 0))],
        out_specs=pl.BlockSpec((tr, d), lambda i: (i, 0)),
        out_shape=jax.ShapeDtypeStruct((t + lc, d), BF16),
        compiler_params=_params(("arbitrary",), 40),
        name="norm_modulate",
    )(x, ctx, norm_w.reshape(1, d), mod)


NT_DIMS = (((1,), (1,)), ((), ()))


def _inproj_kernel(h_ref, wa_ref, wb_ref, b_ref, o_ref, *, n_plain):
    def emit(w):
        acc = lax.dot_general(h_ref[...], w.astype(BF16), NT_DIMS, preferred_element_type=F32)
        o_ref[...] = (acc + b_ref[...]).astype(o_ref.dtype)

    j = pl.program_id(1)

    @pl.when(j < n_plain)
    def _():
        emit(wa_ref[...])

    @pl.when(j >= n_plain)
    def _():
        emit(jnp.concatenate([wa_ref[DT_W:, :], wb_ref[...]], axis=0))


def in_proj(h, w_t, b_main, layer):
    m, k = h.shape
    tm = _pick(m, (2112, 1280, 1024, 768, 640, 512, 256, 128))
    tn = 512
    return pl.pallas_call(
        functools.partial(_inproj_kernel, n_plain=_SRC_DT // tn),
        grid=(m // tm, P_COLS // tn),
        in_specs=[pl.BlockSpec((tm, k), lambda i, j: (i, 0), pipeline_mode=pl.Buffered(1)),
                  pl.BlockSpec((None, tn, k), lambda i, j: (layer, j, 0)),
                  pl.BlockSpec((None, DT_W, k), lambda i, j: (layer, (j + 1) * (tn // DT_W), 0)),
                  pl.BlockSpec((1, tn), lambda i, j: (0, j))],
        out_specs=pl.BlockSpec((tm, tn), lambda i, j: (i, j)),
        out_shape=jax.ShapeDtypeStruct((m, P_COLS), BF16),
        compiler_params=_params(("arbitrary", "arbitrary"), 52),
        name="in_proj",
    )(h, w_t, w_t, b_main.reshape(1, P_COLS))


def _dtproj_kernel(h_ref, w_ref, b_ref, o_ref):
    acc = lax.dot_general(h_ref[...], w_ref[...].astype(BF16), NT_DIMS, preferred_element_type=F32)
    o_ref[...] = acc + b_ref[...]


def dt_proj(h, w_t, b_dt, layer):
    m, k = h.shape
    tm = _pick(m, (1408, 1280, 1024, 768, 640, 512, 256, 128))
    return pl.pallas_call(
        _dtproj_kernel,
        grid=(m // tm,),
        in_specs=[pl.BlockSpec((tm, k), lambda i: (i, 0)),
                  pl.BlockSpec((None, LANES, k), lambda i: (layer, _SRC_DT // LANES, 0)),
                  pl.BlockSpec((1, LANES), lambda i: (0, 0))],
        out_specs=pl.BlockSpec((tm, LANES), lambda i: (i, 0)),
        out_shape=jax.ShapeDtypeStruct((m, LANES), F32),
        compiler_params=_params(("arbitrary",), 48),
        name="dt_proj",
    )(h, w_t, b_dt.reshape(1, LANES))


def _conv_kernel(prev_ref, cur_ref, next_ref, w_ref, b_ref, o_ref, *, n_lat_chunks, n_chunks):
    c = pl.program_id(0)
    has_prev = jnp.logical_and(c != 0, c != n_lat_chunks)
    has_next = jnp.logical_and(c != n_lat_chunks - 1, c != n_chunks - 1)
    pad = CONV_K // 2
    cw = 512
    for jb in range(XBC_W // cw):
        sl = slice(jb * cw, (jb + 1) * cw)
        cur = cur_ref[:, sl].astype(F32)
        pv = prev_ref[SSM_CHUNK - 16:SSM_CHUNK, sl].astype(F32)[8:16]
        nx = next_ref[0:16, sl].astype(F32)[0:8]
        pv = jnp.where(has_prev, pv, 0.0)
        nx = jnp.where(has_next, nx, 0.0)
        ext = jnp.concatenate([pv, cur, nx], axis=0)
        acc = jnp.broadcast_to(b_ref[:, sl], (SSM_CHUNK, cw))
        n_ext = SSM_CHUNK + 16
        for kk in range(CONV_K):
            tap = ext if kk == pad else pltpu.roll(ext, (pad - kk) % n_ext, 0)
            acc = acc + w_ref[kk:kk + 1, sl] * tap[8:8 + SSM_CHUNK]
        o_ref[:, sl] = _silu(acc).astype(o_ref.dtype)


def ssd_conv(p, conv_w, conv_b, n_lat_chunks):
    r = p.shape[0]
    n_chunks = r // SSM_CHUNK

    def prev_map(c):
        bad = jnp.logical_or(c == 0, c == n_lat_chunks)
        return (jnp.where(bad, c, c - 1), 0)

    def next_map(c):
        bad = jnp.logical_or(c == n_lat_chunks - 1, c == n_chunks - 1)
        return (jnp.where(bad, c, c + 1), 0)

    blk = (SSM_CHUNK, XBC_W)
    return pl.pallas_call(
        functools.partial(_conv_kernel, n_lat_chunks=n_lat_chunks, n_chunks=n_chunks),
        grid=(n_chunks,),
        in_specs=[pl.BlockSpec(blk, prev_map),
                  pl.BlockSpec(blk, lambda c: (c, 0)),
                  pl.BlockSpec(blk, next_map),
                  pl.BlockSpec((8, XBC_W), lambda c: (0, 0)),
                  pl.BlockSpec((1, XBC_W), lambda c: (0, 0))],
        out_specs=pl.BlockSpec(blk, lambda c: (c, 0)),
        out_shape=jax.ShapeDtypeStruct((r, XBC_W), BF16),
        compiler_params=_params(("arbitrary",), 32),
        name="ssd_conv",
    )(p, p, p, jnp.pad(conv_w, ((0, 8 - CONV_K), (0, 0))), conv_b.reshape(1, XBC_W))


def _dot3(tri, v):
    hi = v.astype(BF16)
    r1 = v - hi.astype(F32)
    mid = r1.astype(BF16)
    lo = (r1 - mid.astype(F32)).astype(BF16)
    return (jnp.dot(tri, hi, preferred_element_type=F32)
            + jnp.dot(tri, mid, preferred_element_type=F32)
            + jnp.dot(tri, lo, preferred_element_type=F32))


def _pair_cols(v, ha, lane_lo):
    rows = v.shape[0]
    a = jnp.broadcast_to(v[:, ha:ha + 1], (rows, LANES))
    b = jnp.broadcast_to(v[:, ha + 1:ha + 2], (rows, LANES))
    return jnp.where(lane_lo, a, b)


SSD_STEP_CHUNKS = 2


def _ssd_chunk(u_ref, dt_ref, cst_ref, spread_ref, ht_ref, emit, rs, *, reverse):
    L = SSM_CHUNK
    hoff = SSM_HEADS if reverse else 0
    li = lax.broadcasted_iota(jnp.int32, (L, L), 0)
    si = lax.broadcasted_iota(jnp.int32, (L, L), 1)
    keep = (li <= si) if reverse else (li >= si)
    tri = jnp.where((si >= li) if reverse else (si <= li), 1.0, 0.0).astype(BF16)
    lane_lo = lax.broadcasted_iota(jnp.int32, (L, LANES), 1) < SSM_HEAD_DIM
    lane_lo1 = lax.broadcasted_iota(jnp.int32, (1, LANES), 1) < SSM_HEAD_DIM

    raw = dt_ref[rs, :] + cst_ref[0:1, :]
    dtv = jnp.maximum(raw, 0.0) + jnp.log1p(jnp.exp(-jnp.abs(raw)))
    da = dtv * cst_ref[1:2, :]
    acs = _dot3(tri, da)
    acs_t = acs.T
    dt_t = dtv.T
    a_last = acs[0:1, :] if reverse else acs[L - 1:L, :]
    e_last = jnp.exp(a_last)
    per_head = jnp.concatenate([jnp.exp(a_last - acs) * dtv, jnp.exp(acs)], axis=0).astype(BF16)

    pairs = GROUP_W // LANES
    for g in range(SSM_GROUPS):
        bg = u_ref[rs, BRANCH_W + g * SSM_STATE:BRANCH_W + (g + 1) * SSM_STATE]
        cg = u_ref[rs, BRANCH_W + SSM_GN + g * SSM_STATE:BRANCH_W + SSM_GN + (g + 1) * SSM_STATE]
        cb = lax.dot_general(cg, bg, NT_DIMS, preferred_element_type=F32)
        bg_t = bg.astype(F32).T.astype(BF16)
        spread = jnp.dot(per_head, spread_ref[g], preferred_element_type=F32)
        y_parts, d_parts = [], []
        for pi in range(pairs):
            ha = hoff + g * (2 * pairs) + 2 * pi
            xp = u_ref[rs, g * GROUP_W + pi * LANES:g * GROUP_W + (pi + 1) * LANES]
            zero = jnp.zeros_like(xp)
            y_pair = None
            for hh, xm in ((ha, jnp.where(lane_lo, xp, zero)), (ha + 1, jnp.where(lane_lo, zero, xp))):
                diff = acs[:, hh:hh + 1] - acs_t[hh:hh + 1, :]
                dec = jnp.exp(jnp.where(keep, diff, -jnp.inf))
                m = (cb * dec * dt_t[hh:hh + 1, :]).astype(BF16)
                yh = jnp.dot(m, xm, preferred_element_type=F32)
                y_pair = yh if y_pair is None else y_pair + yh
            y_parts.append(y_pair)
            d_parts.append(_pair_cols(e_last, ha, lane_lo1))
        xw = (u_ref[rs, g * GROUP_W:(g + 1) * GROUP_W].astype(F32) * spread[:L]).astype(BF16)
        hs = ht_ref[g]
        y_off = jnp.dot(cg, hs.astype(BF16), preferred_element_type=F32) * spread[L:]
        s_new = jnp.dot(bg_t, xw, preferred_element_type=F32)
        ht_ref[g] = hs * jnp.concatenate(d_parts, axis=1) + s_new
        emit(g, jnp.concatenate(y_parts, axis=1) + y_off)


def _ssd_fwd_kernel(u_ref, dt_ref, cst_ref, spread_ref, o_ref, ht_ref):
    @pl.when(pl.program_id(0) == 0)
    def _():
        ht_ref[...] = jnp.zeros(ht_ref.shape, F32)

    for ci in range(SSD_STEP_CHUNKS):
        rs = slice(ci * SSM_CHUNK, (ci + 1) * SSM_CHUNK)

        def emit(g, y, rs=rs):
            o_ref[rs, g * GROUP_W:(g + 1) * GROUP_W] = y

        _ssd_chunk(u_ref, dt_ref, cst_ref, spread_ref, ht_ref, emit, rs, reverse=False)


def _ssd_bwd_kernel(u_ref, dt_ref, cst_ref, spread_ref, yf_ref, z0_ref, z1_ref, dsk_ref, nw_ref, o_ref, ht_ref, y_ref):
    @pl.when(pl.program_id(0) == 0)
    def _():
        ht_ref[...] = jnp.zeros(ht_ref.shape, F32)

    half = SSM_GROUPS // 2

    for ci in reversed(range(SSD_STEP_CHUNKS)):
        rs = slice(ci * SSM_CHUNK, (ci + 1) * SSM_CHUNK)

        def emit(g, y, rs=rs):
            sl = slice(g * GROUP_W, (g + 1) * GROUP_W)
            z_ref = z0_ref if g < half else z1_ref
            zsl = slice((g % half) * GROUP_W, (g % half + 1) * GROUP_W)
            tot = y + yf_ref[rs, sl] + dsk_ref[:, sl] * u_ref[rs, sl].astype(F32)
            y_ref[rs, sl] = tot * _silu(z_ref[rs, zsl].astype(F32))

        _ssd_chunk(u_ref, dt_ref, cst_ref, spread_ref, ht_ref, emit, rs, reverse=True)
    yv = y_ref[...]
    o_ref[...] = (yv * lax.rsqrt(jnp.mean(yv * yv, axis=-1, keepdims=True) + EPS) * nw_ref[...]).astype(o_ref.dtype)


def ssd_branch(u, dt_raw, cst, p, d_exp, ssm_norm_w, n_ctx_rows):
    r = u.shape[0]
    step_rows = SSD_STEP_CHUNKS * SSM_CHUNK
    n_chunks = r // step_rows
    n_ctx_chunks = n_ctx_rows // step_rows
    n_lat = n_chunks - n_ctx_chunks
    state = pltpu.VMEM((SSM_GROUPS, SSM_STATE, GROUP_W), F32)

    def fwd_idx(i):
        return jnp.where(i < n_ctx_chunks, n_lat + i, i - n_ctx_chunks)

    def bwd_idx(i):
        return n_chunks - 1 - i

    heads_per_group = SSM_HEADS // SSM_GROUPS
    hh = np.arange(LANES)[None, None, :, None]
    owner = (np.arange(2)[:, None, None, None] * SSM_HEADS + np.arange(SSM_GROUPS)[None, :, None, None] * heads_per_group
             + np.arange(GROUP_W)[None, None, None, :] // SSM_HEAD_DIM)
    spread = jnp.asarray(hh == owner, BF16)

    def spread_spec(direction):
        return pl.BlockSpec((None, SSM_GROUPS, LANES, GROUP_W), lambda i: (direction, 0, 0, 0))

    ublk = (step_rows, XBC_W)
    yf = pl.pallas_call(
        _ssd_fwd_kernel,
        grid=(n_chunks,),
        in_specs=[pl.BlockSpec(ublk, lambda i: (fwd_idx(i), 0)),
                  pl.BlockSpec((step_rows, LANES), lambda i: (fwd_idx(i), 0)),
                  pl.BlockSpec((8, LANES), lambda i: (0, 0)),
                  spread_spec(0)],
        out_specs=pl.BlockSpec((step_rows, BRANCH_W), lambda i: (fwd_idx(i), 0)),
        out_shape=jax.ShapeDtypeStruct((r, BRANCH_W), F32),
        scratch_shapes=[state],
        compiler_params=_params(("arbitrary",), 32),
        name="ssd_fwd",
    )(u, dt_raw, cst, spread)

    zw = BRANCH_W // 2
    z0 = OFF_Z // zw
    return pl.pallas_call(
        _ssd_bwd_kernel,
        grid=(n_chunks,),
        in_specs=[pl.BlockSpec(ublk, lambda i: (bwd_idx(i), 0)),
                  pl.BlockSpec((step_rows, LANES), lambda i: (bwd_idx(i), 0)),
                  pl.BlockSpec((8, LANES), lambda i: (0, 0)),
                  spread_spec(1),
                  pl.BlockSpec((step_rows, BRANCH_W), lambda i: (bwd_idx(i), 0)),
                  pl.BlockSpec((step_rows, zw), lambda i: (bwd_idx(i), z0)),
                  pl.BlockSpec((step_rows, zw), lambda i: (bwd_idx(i), z0 + 1)),
                  pl.BlockSpec((1, BRANCH_W), lambda i: (0, 0)),
                  pl.BlockSpec((1, BRANCH_W), lambda i: (0, 0))],
        out_specs=pl.BlockSpec((step_rows, BRANCH_W), lambda i: (bwd_idx(i), 0)),
        out_shape=jax.ShapeDtypeStruct((r, BRANCH_W), BF16),
        scratch_shapes=[state, pltpu.VMEM((step_rows, BRANCH_W), F32)],
        compiler_params=_params(("arbitrary",), 32),
        name="ssd_bwd",
    )(u, dt_raw, cst, spread, yf, p, p, d_exp, ssm_norm_w.reshape(1, BRANCH_W))


def _qkprep_kernel(qk_ref, cos_ref, sin_ref, qw_ref, kw_ref, swap_ref, q_ref, k_ref):
    cosv = cos_ref[...]
    sinv = sin_ref[...]
    scale = HEAD_DIM ** -0.5 * LOG2E
    swap = swap_ref[...]
    for h in range(ATT_HEADS + KV_HEADS):
        t = qk_ref[:, h * HEAD_DIM:(h + 1) * HEAD_DIM].astype(F32)
        w = qw_ref[...] if h < ATT_HEADS else kw_ref[...]
        n = t * lax.rsqrt(jnp.mean(t * t, axis=-1, keepdims=True) + EPS) * w
        partner = jnp.dot(n.astype(BF16), swap, preferred_element_type=F32)
        rot = n * cosv + partner * sinv
        if h < ATT_HEADS:
            q_ref[:, h * HEAD_DIM:(h + 1) * HEAD_DIM] = (rot * scale).astype(q_ref.dtype)
        else:
            hk = h - ATT_HEADS
            k_ref[:, hk * HEAD_DIM:(hk + 1) * HEAD_DIM] = rot.astype(k_ref.dtype)


def qk_prep(p, cos_t, sin_t, qw, kw):
    r = p.shape[0]
    tr = 256
    kvw = KV_HEADS * HEAD_DIM
    lane = np.arange(HEAD_DIM)
    swap = jnp.asarray(lane[:, None] == (lane ^ 1)[None, :], BF16)
    return pl.pallas_call(
        _qkprep_kernel,
        grid=(r // tr,),
        in_specs=[pl.BlockSpec((tr, QK_W), lambda i: (i, OFF_QA // QK_W)),
                  pl.BlockSpec((tr, HEAD_DIM), lambda i: (i, 0)),
                  pl.BlockSpec((tr, HEAD_DIM), lambda i: (i, 0)),
                  pl.BlockSpec((1, HEAD_DIM), lambda i: (0, 0)),
                  pl.BlockSpec((1, HEAD_DIM), lambda i: (0, 0)),
                  pl.BlockSpec((HEAD_DIM, HEAD_DIM), lambda i: (0, 0))],
        out_specs=[pl.BlockSpec((tr, BRANCH_W), lambda i: (i, 0)),
                   pl.BlockSpec((tr, kvw), lambda i: (i, 0))],
        out_shape=[jax.ShapeDtypeStruct((r, BRANCH_W), BF16),
                   jax.ShapeDtypeStruct((r, kvw), BF16)],
        compiler_params=_params(("arbitrary",), 32),
        name="qk_prep",
    )(p, cos_t, sin_t, qw, kw, swap)


def _flash_kernel(q_ref, k_ref, v_ref, g_ref, o_ref, m_ref, l_ref, acc_ref, *, nk):
    ki = pl.program_id(2)

    @pl.when(ki == 0)
    def _():
        m_ref[...] = jnp.full(m_ref.shape, -jnp.inf, F32)
        l_ref[...] = jnp.zeros(l_ref.shape, F32)
        acc_ref[...] = jnp.zeros(acc_ref.shape, F32)

    k = k_ref[...]
    v1 = jnp.concatenate([v_ref[...], jnp.ones(v_ref.shape, v_ref.dtype)], axis=1)
    reps = k.shape[0] // LANES
    for r in range(ATT_GROUP):
        hs = slice(r * HEAD_DIM, (r + 1) * HEAD_DIM)
        s = lax.dot_general(q_ref[:, hs], k, NT_DIMS, preferred_element_type=F32)
        m_prev = m_ref[r]
        m_next = jnp.maximum(m_prev, jnp.max(s, axis=1, keepdims=True))
        p = jnp.exp2(s - jnp.concatenate([m_next] * reps, axis=1))
        alpha = jnp.exp2(m_prev - m_next)
        pv = jnp.dot(p.astype(v1.dtype), v1, preferred_element_type=F32)
        l_ref[r] = alpha * l_ref[r] + pv[:, HEAD_DIM:]
        acc_ref[r] = alpha * acc_ref[r] + pv[:, :HEAD_DIM]
        m_ref[r] = m_next

    @pl.when(ki == nk - 1)
    def _():
        for r in range(ATT_GROUP):
            hs = slice(r * HEAD_DIM, (r + 1) * HEAD_DIM)
            o = acc_ref[r] / l_ref[r]
            o_ref[:, hs] = (_silu(g_ref[:, hs].astype(F32)) * o).astype(o_ref.dtype)


def flash_gqa(qn, kn, p, *, q_rows, q_row0, k_rows, k_row0, tq, tk):
    nq = q_rows // tq
    nk = k_rows // tk
    qb0 = q_row0 // tq
    kb0 = k_row0 // tk
    gw = ATT_GROUP * HEAD_DIM
    return pl.pallas_call(
        functools.partial(_flash_kernel, nk=nk),
        grid=(KV_HEADS, nq, nk),
        in_specs=[pl.BlockSpec((tq, gw), lambda g, i, j: (qb0 + i, g)),
                  pl.BlockSpec((tk, HEAD_DIM), lambda g, i, j: (kb0 + j, g)),
                  pl.BlockSpec((tk, HEAD_DIM), lambda g, i, j: (kb0 + j, OFF_VA // HEAD_DIM + g)),
                  pl.BlockSpec((tq, gw), lambda g, i, j: (qb0 + i, OFF_GA // gw + g))],
        out_specs=pl.BlockSpec((tq, gw), lambda g, i, j: (i, g)),
        out_shape=jax.ShapeDtypeStruct((q_rows, BRANCH_W), BF16),
        scratch_shapes=[pltpu.VMEM((ATT_GROUP, tq, LANES), F32),
                        pltpu.VMEM((ATT_GROUP, tq, LANES), F32),
                        pltpu.VMEM((ATT_GROUP, tq, HEAD_DIM), F32)],
        compiler_params=_params(("arbitrary", "arbitrary", "arbitrary"), 48),
        name="flash_gqa",
    )(qn, kn, p, p)


NA_QROWS = 4
NA_KROWS = 12
NA_PATTERNS = 3


def _na_rel_row(pattern, qr, kr):
    if pattern == 0:
        return kr - qr + (NA_WIN_R - 1) if kr < NA_WIN_R else None
    if pattern == 1:
        return kr - qr + (NA_WIN_R - 1 - NA_WIN_R // 2) if 0 <= kr - qr < NA_WIN_R else None
    first = NA_KROWS - NA_WIN_R
    return kr - qr + (NA_WIN_R - 1) - (NA_KROWS - NA_QROWS) if kr >= first else None


def _na_kernel(q_ref, k_ref, v_ref, g_ref, tp_ref, o_ref, bias_ref, qs_ref, v1_ref, *, n_lat, n_ctx, grid_rows,
               prep_rows):
    qn = NA_QROWS * GRID_W
    win = NA_KROWS * GRID_W
    n_blocks = grid_rows // NA_QROWS
    rows = n_lat + n_ctx
    for r0 in range(0, rows, prep_rows):
        rs = slice(r0, r0 + prep_rows)
        qs_ref[rs, :] = (q_ref[rs, :].astype(F32) * (HEAD_DIM ** -0.5 * LOG2E)).astype(BF16)
        v1_ref[rs, 0:HEAD_DIM] = v_ref[rs, :]
        v1_ref[rs, HEAD_DIM:2 * HEAD_DIM] = jnp.ones((prep_rows, HEAD_DIM), BF16)
    kc = k_ref[n_lat:rows, :]
    vc = v1_ref[n_lat:rows, :]

    lane_lo = lax.broadcasted_iota(jnp.int32, (GRID_W, LANES), 1) < GRID_W
    for pattern in range(NA_PATTERNS):
        for qr in range(NA_QROWS):
            for kp in range(NA_KROWS // 2):
                ja = _na_rel_row(pattern, qr, 2 * kp)
                jb = _na_rel_row(pattern, qr, 2 * kp + 1)
                if ja is None and jb is None:
                    tile = jnp.full((GRID_W, LANES), NEG_BIAS, F32)
                elif jb is None:
                    tile = jnp.where(lane_lo, tp_ref[ja + 1], NEG_BIAS)
                elif ja is None:
                    tile = jnp.where(lane_lo, NEG_BIAS, tp_ref[jb])
                else:
                    tile = tp_ref[jb]
                bias_ref[pattern, qr * GRID_W:(qr + 1) * GRID_W, kp * LANES:(kp + 1) * LANES] = tile

    def block_body(b, carry):
        u0 = jnp.clip(b * NA_QROWS - NA_WIN_R // 2, 0, grid_rows - NA_KROWS)
        pattern = jnp.where(b == 0, 0, jnp.where(b == n_blocks - 1, 2, 1))
        qs = pl.ds(pl.multiple_of(b * qn, qn), qn)
        ws = pl.ds(pl.multiple_of(u0 * GRID_W, GRID_W), win)
        q = qs_ref[qs, :]
        s_loc = lax.dot_general(q, k_ref[ws, :], NT_DIMS, preferred_element_type=F32) + bias_ref[pattern]
        s_ctx = lax.dot_general(q, kc, NT_DIMS, preferred_element_type=F32)
        m = jnp.maximum(jnp.max(s_loc, axis=1, keepdims=True), jnp.max(s_ctx, axis=1, keepdims=True))
        pv = (jnp.dot(jnp.exp2(s_loc - m).astype(BF16), v1_ref[ws, :], preferred_element_type=F32)
              + jnp.dot(jnp.exp2(s_ctx - m).astype(BF16), vc, preferred_element_type=F32))
        o = pv[:, :HEAD_DIM] / pv[:, HEAD_DIM:]
        o_ref[qs, :] = (_silu(g_ref[qs, :].astype(F32)) * o).astype(o_ref.dtype)
        return carry

    lax.fori_loop(0, n_blocks, block_body, 0, unroll=8)

    qs = slice(n_lat, rows)
    s = lax.dot_general(qs_ref[qs, :], kc, NT_DIMS, preferred_element_type=F32)
    pvc = jnp.dot(jnp.exp2(s - jnp.max(s, axis=1, keepdims=True)).astype(BF16), vc, preferred_element_type=F32)
    oc = pvc[:, :HEAD_DIM] / pvc[:, HEAD_DIM:]
    o_ref[qs, :] = (_silu(g_ref[qs, :].astype(F32)) * oc).astype(o_ref.dtype)


def neighbourhood(p, tp, n_lat):
    r = p.shape[0]
    n_ctx = r - n_lat
    col = lambda off: (lambda h: (0, off // HEAD_DIM + h))
    return pl.pallas_call(
        functools.partial(_na_kernel, n_lat=n_lat, n_ctx=n_ctx, grid_rows=n_lat // GRID_W,
                          prep_rows=_pick(r, (1056, 640, 256, 128))),
        grid=(NA_HEADS,),
        in_specs=[pl.BlockSpec((r, HEAD_DIM), col(OFF_QN)),
                  pl.BlockSpec((r, HEAD_DIM), col(OFF_KN)),
                  pl.BlockSpec((r, HEAD_DIM), col(OFF_VN)),
                  pl.BlockSpec((r, HEAD_DIM), col(OFF_GN)),
                  pl.BlockSpec((None, 2 * NA_WIN_R, GRID_W, 2 * GRID_W), lambda h: (h, 0, 0, 0))],
        out_specs=pl.BlockSpec((r, HEAD_DIM), lambda h: (0, h)),
        out_shape=jax.ShapeDtypeStruct((r, BRANCH_W), BF16),
        scratch_shapes=[pltpu.VMEM((NA_PATTERNS, NA_QROWS * GRID_W, NA_KROWS * GRID_W), F32),
                        pltpu.VMEM((r, HEAD_DIM), BF16),
                        pltpu.VMEM((r, 2 * HEAD_DIM), BF16)],
        compiler_params=_params(("arbitrary",), 48),
        name="neighbourhood",
    )(p, p, p, p, tp)


def na_bias_table(rpb):
    n_rel = 2 * NA_WIN_C - 1
    c = np.arange(GRID_W)[:, None]
    kc = np.arange(GRID_W)[None, :]
    c0 = np.clip(c - NA_WIN_C // 2, 0, GRID_W - NA_WIN_C)
    valid = (kc >= c0) & (kc < c0 + NA_WIN_C)
    rel = kc - c + (NA_WIN_C - 1)
    onehot = (rel[None] == np.arange(n_rel)[:, None, None]) & valid[None]
    h, nr, _ = rpb.shape
    t = jnp.dot(rpb.astype(F32).reshape(h * nr, n_rel), jnp.asarray(onehot.reshape(n_rel, -1), F32),
                precision=lax.Precision.HIGHEST).reshape(h, nr, GRID_W, GRID_W)
    t = jnp.where(jnp.asarray(valid), t * LOG2E, NEG_BIAS)
    masked = jnp.full((h, 1, GRID_W, GRID_W), NEG_BIAS, F32)
    tpad = jnp.concatenate([masked, t, masked], axis=1)
    return jnp.concatenate([tpad[:, :-1], tpad[:, 1:]], axis=-1)


def _gateup_kernel(a_ref, b_ref, c_ref, wa_ref, wb_ref, wc_ref, ga_ref, gb_ref, gc_ref, o_ref):
    acc = None
    for br_ref, w_ref, g_ref in ((a_ref, wa_ref, ga_ref), (b_ref, wb_ref, gb_ref), (c_ref, wc_ref, gc_ref)):
        up = jnp.dot(br_ref[...], w_ref[...], preferred_element_type=F32)
        term = _sigmoid(g_ref[...].astype(F32)) * up
        acc = term if acc is None else acc + term
    o_ref[...] = acc.astype(o_ref.dtype)


def gate_up(br_a, br_b, br_c, w_up, p, layer, *, rows, row0s, p_row0):
    tm = _pick(rows, (1024, 512, 256))
    tn = 512

    def br_spec(row0):
        return pl.BlockSpec((tm, BRANCH_W), lambda i, j: (row0 // tm + i, 0))

    def w_spec(b):
        return pl.BlockSpec((None, None, BRANCH_W, tn), lambda i, j: (layer, b, 0, j))

    def g_spec(b):
        return pl.BlockSpec((tm, tn), lambda i, j: (p_row0 // tm + i, (OFF_GM + b * D_MODEL) // tn + j))

    return pl.pallas_call(
        _gateup_kernel,
        grid=(rows // tm, D_MODEL // tn),
        in_specs=[br_spec(row0s[0]), br_spec(row0s[1]), br_spec(row0s[2]),
                  w_spec(0), w_spec(1), w_spec(2), g_spec(0), g_spec(1), g_spec(2)],
        out_specs=pl.BlockSpec((tm, tn), lambda i, j: (i, j)),
        out_shape=jax.ShapeDtypeStruct((rows, D_MODEL), BF16),
        compiler_params=_params(("arbitrary", "arbitrary"), 56),
        name="gate_up",
    )(br_a, br_b, br_c, w_up, w_up, w_up, p, p, p)


def _outproj_kernel(s_ref, w_ref, x_ref, mod_ref, o_ref, *, mod_row):
    y = jnp.dot(s_ref[...], w_ref[...], preferred_element_type=F32)
    o_ref[...] = x_ref[...] + mod_ref[mod_row:mod_row + 1, :] * y


def out_proj(s, w_out, x, mod, layer, mod_row):
    rows = s.shape[0]
    tm = _pick(rows, (1024, 512, 256))
    tn = 512
    return pl.pallas_call(
        functools.partial(_outproj_kernel, mod_row=mod_row),
        grid=(rows // tm, D_MODEL // tn),
        in_specs=[pl.BlockSpec((tm, D_MODEL), lambda i, j: (i, 0)),
                  pl.BlockSpec((None, D_MODEL, tn), lambda i, j: (layer, 0, j)),
                  pl.BlockSpec((tm, tn), lambda i, j: (i, j)),
                  pl.BlockSpec((8, tn), lambda i, j: (0, 2 * D_MODEL // tn + j))],
        out_specs=pl.BlockSpec((tm, tn), lambda i, j: (i, j)),
        out_shape=jax.ShapeDtypeStruct((rows, D_MODEL), F32),
        compiler_params=_params(("arbitrary", "arbitrary"), 48),
        name="out_proj",
    )(s, w_out, x, mod)


def _final_norm_kernel(x_ref, w_ref, o_ref):
    xv = x_ref[...]
    o_ref[...] = xv * lax.rsqrt(jnp.mean(xv * xv, axis=-1, keepdims=True) + EPS) * w_ref[...]


def final_norm(x, w):
    t, d = x.shape
    tr = 256
    return pl.pallas_call(
        _final_norm_kernel,
        grid=(t // tr,),
        in_specs=[pl.BlockSpec((tr, d), lambda i: (i, 0)), pl.BlockSpec((1, d), lambda i: (0, 0))],
        out_specs=pl.BlockSpec((tr, d), lambda i: (i, 0)),
        out_shape=jax.ShapeDtypeStruct((t, d), F32),
        compiler_params=_params(("arbitrary",), 40),
        name="final_norm",
    )(x, w.reshape(1, d))


def _rope_tables(n_lat, n_ctx):
    pos = jnp.arange(n_lat, dtype=jnp.int32)
    row = (pos // GRID_W).astype(F32)
    colp = (pos % GRID_W).astype(F32)
    n_freq = HEAD_DIM // 4
    inv_freq = ROPE_THETA ** (-jnp.arange(n_freq, dtype=F32) / n_freq)
    ang = jnp.concatenate([row[:, None] * inv_freq, colp[:, None] * inv_freq], axis=-1)
    cos, sin = jnp.cos(ang), jnp.sin(ang)
    cos_t = jnp.repeat(cos, 2, axis=-1)
    sin_t = jnp.stack([-sin, sin], axis=-1).reshape(n_lat, HEAD_DIM)
    cos_t = jnp.concatenate([cos_t, jnp.ones((n_ctx, HEAD_DIM), F32)], axis=0)
    sin_t = jnp.concatenate([sin_t, jnp.zeros((n_ctx, HEAD_DIM), F32)], axis=0)
    return cos_t, sin_t


def hybrid_layer(x, ctx, sb, lp, cos_t, sin_t, update_ctx):
    t = x.shape[0]
    lc = ctx.shape[0]
    layer = lp["layer"]
    mod = modulation(sb, lp["w_mod"], lp["b_mod"], layer)
    h = norm_modulate(x, ctx, lp["norm_w"], mod)
    p = in_proj(h, lp["w_in_t"], lp["b_main"], layer)
    dt_raw = dt_proj(h, lp["w_in_t"], lp["b_dt"], layer)

    u = ssd_conv(p, lp["conv_w"], lp["conv_b"], t // SSM_CHUNK)
    br_a = ssd_branch(u, dt_raw, lp["ssd_cst"], p, lp["d_exp"], lp["ssm_norm_w"], lc)

    qn, kn = qk_prep(p, cos_t, sin_t, lp["q_norm_w"], lp["k_norm_w"])
    tq = _pick(t, (1024, 512, 256))
    tk = _pick(t + lc, (1408, 768, 640, 512, 384, 256))
    br_b = flash_gqa(qn, kn, p, q_rows=t, q_row0=0, k_rows=t + lc, k_row0=0, tq=tq, tk=tk)

    br_c = neighbourhood(p, lp["na_bias"], t)

    s = gate_up(br_a, br_b, br_c, lp["w_up"], p, layer, rows=t, row0s=(0, 0, 0), p_row0=0)
    x_new = out_proj(s, lp["w_out"], x, mod, layer, 0)
    if not update_ctx:
        return x_new, ctx
    br_b_c = flash_gqa(qn, kn, p, q_rows=lc, q_row0=t, k_rows=lc, k_row0=t, tq=lc, tk=lc)
    s_c = gate_up(br_a, br_b_c, br_c, lp["w_up"], p, layer, rows=lc, row0s=(t, 0, t), p_row0=t)
    ctx_new = out_proj(s_c, lp["w_out"], ctx, mod, layer, 1)
    return x_new, ctx_new


def layer_params(layer, w_mod, b_mod, norm_w, w_in_t, b_in, conv_w, conv_b, a_log, dt_bias, d_skip, ssm_norm_w,
                 q_norm_w, k_norm_w, rpb, w_up, w_out):
    pad_heads = [(0, 0), (0, LANES - DT_W)]
    b = b_in[layer].astype(F32)
    cst = jnp.concatenate([
        jnp.pad(dt_bias[layer].astype(F32).reshape(1, DT_W), pad_heads),
        jnp.pad(-jnp.exp(a_log[layer].astype(F32)).reshape(1, DT_W), pad_heads),
        jnp.zeros((6, LANES), F32)], axis=0)
    return dict(
        layer=layer, w_mod=w_mod, b_mod=b_mod[layer], norm_w=norm_w[layer], w_in_t=w_in_t,
        b_main=jnp.concatenate([b[:_SRC_DT], b[_SRC_QA:]]),
        b_dt=jnp.pad(b[_SRC_DT:_SRC_QA], (0, LANES - DT_W)),
        conv_w=conv_w[layer], conv_b=conv_b[layer], ssd_cst=cst,
        d_exp=jnp.repeat(d_skip[layer].astype(F32), SSM_HEAD_DIM).reshape(1, BRANCH_W),
        ssm_norm_w=ssm_norm_w[layer],
        q_norm_w=q_norm_w[layer].reshape(1, HEAD_DIM), k_norm_w=k_norm_w[layer].reshape(1, HEAD_DIM),
        na_bias=na_bias_table(rpb[layer]), w_up=w_up, w_out=w_out)


def kernel(x, c, ctx, c_ctx, norm_w, w_mod, b_mod, w_in, b_in, conv_w, conv_b, a_log, dt_bias, d_skip, ssm_norm_w, q_norm_w, k_norm_w, rpb, w_up, w_out, final_norm_w):
    xs = x[0]
    cs = ctx[0]
    sb = jnp.broadcast_to(jnp.stack([c[0], c_ctx])[:, :, None], (2, D_MODEL, LANES))
    cos_t, sin_t = _rope_tables(xs.shape[0], cs.shape[0])
    w_in_t = jnp.swapaxes(w_in, 1, 2)
    w_up_b = w_up.astype(BF16)
    w_out_b = w_out.astype(BF16)
    for layer in range(DEPTH):
        lp = layer_params(layer, w_mod, b_mod, norm_w, w_in_t, b_in, conv_w, conv_b, a_log, dt_bias, d_skip,
                          ssm_norm_w, q_norm_w, k_norm_w, rpb, w_up_b, w_out_b)
        xs, cs = hybrid_layer(xs, cs, sb, lp, cos_t, sin_t, layer < DEPTH - 1)
    return final_norm(xs, final_norm_w)[None]
```

```python
import functools
import math

import numpy as np
import jax
import jax.numpy as jnp
from jax import lax
from jax.experimental import pallas as pl
from jax.experimental.pallas import tpu as pltpu

F32 = jnp.float32
BF16 = jnp.bfloat16

D_MODEL = 4096
DEPTH = 2
GRID_W = 64
N_BRANCH = 3
BRANCH_W = 2048
EPS = 1e-6

SSM_HEADS = 32
SSM_HEAD_DIM = 64
SSM_GROUPS = 4
SSM_STATE = 128
SSM_CHUNK = 128
CONV_K = 5
SSM_GN = SSM_GROUPS * SSM_STATE
XBC_W = BRANCH_W + 2 * SSM_GN
DT_W = 2 * SSM_HEADS
GROUP_W = (SSM_HEADS // SSM_GROUPS) * SSM_HEAD_DIM

ATT_HEADS = 16
KV_HEADS = 4
ATT_GROUP = ATT_HEADS // KV_HEADS
HEAD_DIM = 128
ROPE_THETA = 10000.0

NA_HEADS = 16
NA_WIN_R = 8
NA_WIN_C = 16

LANES = 128
MIB = 1024 * 1024

_SRC_DT = XBC_W + BRANCH_W
_SRC_QA = _SRC_DT + 2 * SSM_HEADS
OFF_XBC = 0
OFF_Z = OFF_XBC + XBC_W
OFF_QA = OFF_Z + BRANCH_W
OFF_KA = OFF_QA + BRANCH_W
OFF_VA = OFF_KA + KV_HEADS * HEAD_DIM
OFF_GA = OFF_VA + KV_HEADS * HEAD_DIM
OFF_QN = OFF_GA + BRANCH_W
OFF_KN = OFF_QN + BRANCH_W
OFF_VN = OFF_KN + BRANCH_W
OFF_GN = OFF_VN + BRANCH_W
OFF_GM = OFF_GN + BRANCH_W
P_COLS = OFF_GM + N_BRANCH * D_MODEL
QK_W = BRANCH_W + KV_HEADS * HEAD_DIM

NEG_BIAS = -1e30
LOG2E = math.log2(math.e)


def _params(sem, vmem_mib):
    return pltpu.CompilerParams(dimension_semantics=sem, vmem_limit_bytes=vmem_mib * MIB)


def _pick(n, candidates):
    for c in candidates:
        if n % c == 0:
            return c
    raise ValueError(f"no tile for {n} in {candidates}")


def _sigmoid(v):
    return 1.0 / (1.0 + jnp.exp(-v))


def _silu(v):
    return v * _sigmoid(v)


def _mod_kernel(s_ref, w_ref, b_ref, o_ref, act_ref):
    tn = w_ref.shape[1]
    k = w_ref.shape[0]

    @pl.when(pl.program_id(0) == 0)
    def _():
        act_ref[...] = _silu(s_ref[...])

    o_ref[...] = jnp.zeros(o_ref.shape, F32)
    for v in range(2):
        s = act_ref[v]
        for jb in range(tn // LANES):
            sl = slice(jb * LANES, (jb + 1) * LANES)
            prod = w_ref[:, sl] * s
            part = prod.reshape(k // 8, 8, LANES).sum(axis=0)
            o_ref[v:v + 1, sl] = part.sum(axis=0, keepdims=True) + b_ref[:, sl]


def modulation(sb, w_mod, b_mod, layer):
    _, k, n = w_mod.shape
    tn = 512
    return pl.pallas_call(
        _mod_kernel,
        grid=(n // tn,),
        in_specs=[pl.BlockSpec((2, k, LANES), lambda j: (0, 0, 0)),
                  pl.BlockSpec((None, k, tn), lambda j: (layer, 0, j)),
                  pl.BlockSpec((1, tn), lambda j: (0, j))],
        out_specs=pl.BlockSpec((8, tn), lambda j: (0, j)),
        out_shape=jax.ShapeDtypeStruct((8, n), F32),
        scratch_shapes=[pltpu.VMEM((2, k, LANES), F32)],
        compiler_params=_params(("arbitrary",), 40),
        name="modulation",
    )(sb, w_mod, b_mod.reshape(1, n))


def _norm_kernel(x_ref, c_ref, nw_ref, mod_ref, o_ref, *, n_lat_blocks):
    i = pl.program_id(0)

    def emit(src_ref, row):
        xv = src_ref[...]
        y = xv * lax.rsqrt(jnp.mean(xv * xv, axis=-1, keepdims=True) + EPS) * nw_ref[...]
        shift = mod_ref[row:row + 1, 0:D_MODEL]
        scale = mod_ref[row:row + 1, D_MODEL:2 * D_MODEL]
        o_ref[...] = (y * (1.0 + scale) + shift).astype(o_ref.dtype)

    @pl.when(i < n_lat_blocks)
    def _():
        emit(x_ref, 0)

    @pl.when(i >= n_lat_blocks)
    def _():
        emit(c_ref, 1)


def norm_modulate(x, ctx, norm_w, mod):
    t, d = x.shape
    lc = ctx.shape[0]
    tr = 256
    n_lat = t // tr
    n_ctx = lc // tr
    return pl.pallas_call(
        functools.partial(_norm_kernel, n_lat_blocks=n_lat),
        grid=(n_lat + n_ctx,),
        in_specs=[pl.BlockSpec((tr, d), lambda i: (jnp.minimum(i, n_lat - 1), 0)),
                  pl.BlockSpec((tr, d), lambda i: (jnp.maximum(i - n_lat, 0), 0)),
                  pl.BlockSpec((1, d), lambda i: (0, 0)),
                  pl.BlockSpec((8, 3 * d), lambda i: (0, 0))],
        out_specs=pl.BlockSpec((tr, d), lambda i: (i, 0)),
        out_shape=jax.ShapeDtypeStruct((t + lc, d), BF16),
        compiler_params=_params(("arbitrary",), 40),
        name="norm_modulate",
    )(x, ctx, norm_w.reshape(1, d), mod)


NT_DIMS = (((1,), (1,)), ((), ()))


def _inproj_kernel(h_ref, wa_ref, wb_ref, b_ref, o_ref, *, n_plain):
    def emit(w):
        acc = lax.dot_general(h_ref[...], w.astype(BF16), NT_DIMS, preferred_element_type=F32)
        o_ref[...] = (acc + b_ref[...]).astype(o_ref.dtype)

    j = pl.program_id(1)

    @pl.when(j < n_plain)
    def _():
        emit(wa_ref[...])

    @pl.when(j >= n_plain)
    def _():
        emit(jnp.concatenate([wa_ref[DT_W:, :], wb_ref[...]], axis=0))


def in_proj(h, w_t, b_main, layer):
    m, k = h.shape
    tm = _pick(m, (2112, 1280, 1024, 768, 640, 512, 256, 128))
    tn = 512
    return pl.pallas_call(
        functools.partial(_inproj_kernel, n_plain=_SRC_DT // tn),
        grid=(m // tm, P_COLS // tn),
        in_specs=[pl.BlockSpec((tm, k), lambda i, j: (i, 0), pipeline_mode=pl.Buffered(1)),
                  pl.BlockSpec((None, tn, k), lambda i, j: (layer, j, 0)),
                  pl.BlockSpec((None, DT_W, k), lambda i, j: (layer, (j + 1) * (tn // DT_W), 0)),
                  pl.BlockSpec((1, tn), lambda i, j: (0, j))],
        out_specs=pl.BlockSpec((tm, tn), lambda i, j: (i, j)),
        out_shape=jax.ShapeDtypeStruct((m, P_COLS), BF16),
        compiler_params=_params(("arbitrary", "arbitrary"), 52),
        name="in_proj",
    )(h, w_t, w_t, b_main.reshape(1, P_COLS))


def _dtproj_kernel(h_ref, w_ref, b_ref, o_ref):
    acc = lax.dot_general(h_ref[...], w_ref[...].astype(BF16), NT_DIMS, preferred_element_type=F32)
    o_ref[...] = acc + b_ref[...]


def dt_proj(h, w_t, b_dt, layer):
    m, k = h.shape
    tm = _pick(m, (1408, 1280, 1024, 768, 640, 512, 256, 128))
    return pl.pallas_call(
        _dtproj_kernel,
        grid=(m // tm,),
        in_specs=[pl.BlockSpec((tm, k), lambda i: (i, 0)),
                  pl.BlockSpec((None, LANES, k), lambda i: (layer, _SRC_DT // LANES, 0)),
                  pl.BlockSpec((1, LANES), lambda i: (0, 0))],
        out_specs=pl.BlockSpec((tm, LANES), lambda i: (i, 0)),
        out_shape=jax.ShapeDtypeStruct((m, LANES), F32),
        compiler_params=_params(("arbitrary",), 48),
        name="dt_proj",
    )(h, w_t, b_dt.reshape(1, LANES))


def _conv_kernel(prev_ref, cur_ref, next_ref, w_ref, b_ref, o_ref, *, n_lat_chunks, n_chunks):
    c = pl.program_id(0)
    has_prev = jnp.logical_and(c != 0, c != n_lat_chunks)
    has_next = jnp.logical_and(c != n_lat_chunks - 1, c != n_chunks - 1)
    pad = CONV_K // 2
    cw = 512
    for jb in range(XBC_W // cw):
        sl = slice(jb * cw, (jb + 1) * cw)
        cur = cur_ref[:, sl].astype(F32)
        pv = prev_ref[SSM_CHUNK - 16:SSM_CHUNK, sl].astype(F32)[8:16]
        nx = next_ref[0:16, sl].astype(F32)[0:8]
        pv = jnp.where(has_prev, pv, 0.0)
        nx = jnp.where(has_next, nx, 0.0)
        ext = jnp.concatenate([pv, cur, nx], axis=0)
        acc = jnp.broadcast_to(b_ref[:, sl], (SSM_CHUNK, cw))
        n_ext = SSM_CHUNK + 16
        for kk in range(CONV_K):
            tap = ext if kk == pad else pltpu.roll(ext, (pad - kk) % n_ext, 0)
            acc = acc + w_ref[kk:kk + 1, sl] * tap[8:8 + SSM_CHUNK]
        o_ref[:, sl] = _silu(acc).astype(o_ref.dtype)


def ssd_conv(p, conv_w, conv_b, n_lat_chunks):
    r = p.shape[0]
    n_chunks = r // SSM_CHUNK

    def prev_map(c):
        bad = jnp.logical_or(c == 0, c == n_lat_chunks)
        return (jnp.where(bad, c, c - 1), 0)

    def next_map(c):
        bad = jnp.logical_or(c == n_lat_chunks - 1, c == n_chunks - 1)
        return (jnp.where(bad, c, c + 1), 0)

    blk = (SSM_CHUNK, XBC_W)
    return pl.pallas_call(
        functools.partial(_conv_kernel, n_lat_chunks=n_lat_chunks, n_chunks=n_chunks),
        grid=(n_chunks,),
        in_specs=[pl.BlockSpec(blk, prev_map),
                  pl.BlockSpec(blk, lambda c: (c, 0)),
                  pl.BlockSpec(blk, next_map),
                  pl.BlockSpec((8, XBC_W), lambda c: (0, 0)),
                  pl.BlockSpec((1, XBC_W), lambda c: (0, 0))],
        out_specs=pl.BlockSpec(blk, lambda c: (c, 0)),
        out_shape=jax.ShapeDtypeStruct((r, XBC_W), BF16),
        compiler_params=_params(("arbitrary",), 32),
        name="ssd_conv",
    )(p, p, p, jnp.pad(conv_w, ((0, 8 - CONV_K), (0, 0))), conv_b.reshape(1, XBC_W))


def _dot3(tri, v):
    hi = v.astype(BF16)
    r1 = v - hi.astype(F32)
    mid = r1.astype(BF16)
    lo = (r1 - mid.astype(F32)).astype(BF16)
    return (jnp.dot(tri, hi, preferred_element_type=F32)
            + jnp.dot(tri, mid, preferred_element_type=F32)
            + jnp.dot(tri, lo, preferred_element_type=F32))


def _pair_cols(v, ha, lane_lo):
    rows = v.shape[0]
    a = jnp.broadcast_to(v[:, ha:ha + 1], (rows, LANES))
    b = jnp.broadcast_to(v[:, ha + 1:ha + 2], (rows, LANES))
    return jnp.where(lane_lo, a, b)


SSD_STEP_CHUNKS = 2


def _ssd_chunk(u_ref, dt_ref, cst_ref, spread_ref, ht_ref, emit, rs, *, reverse):
    L = SSM_CHUNK
    hoff = SSM_HEADS if reverse else 0
    li = lax.broadcasted_iota(jnp.int32, (L, L), 0)
    si = lax.broadcasted_iota(jnp.int32, (L, L), 1)
    keep = (li <= si) if reverse else (li >= si)
    tri = jnp.where((si >= li) if reverse else (si <= li), 1.0, 0.0).astype(BF16)
    lane_lo = lax.broadcasted_iota(jnp.int32, (L, LANES), 1) < SSM_HEAD_DIM
    lane_lo1 = lax.broadcasted_iota(jnp.int32, (1, LANES), 1) < SSM_HEAD_DIM

    raw = dt_ref[rs, :] + cst_ref[0:1, :]
    dtv = jnp.maximum(raw, 0.0) + jnp.log1p(jnp.exp(-jnp.abs(raw)))
    da = dtv * cst_ref[1:2, :]
    acs = _dot3(tri, da)
    acs_t = acs.T
    dt_t = dtv.T
    a_last = acs[0:1, :] if reverse else acs[L - 1:L, :]
    e_last = jnp.exp(a_last)
    per_head = jnp.concatenate([jnp.exp(a_last - acs) * dtv, jnp.exp(acs)], axis=0).astype(BF16)

    pairs = GROUP_W // LANES
    for g in range(SSM_GROUPS):
        bg = u_ref[rs, BRANCH_W + g * SSM_STATE:BRANCH_W + (g + 1) * SSM_STATE]
        cg = u_ref[rs, BRANCH_W + SSM_GN + g * SSM_STATE:BRANCH_W + SSM_GN + (g + 1) * SSM_STATE]
        cb = lax.dot_general(cg, bg, NT_DIMS, preferred_element_type=F32)
        bg_t = bg.astype(F32).T.astype(BF16)
        spread = jnp.dot(per_head, spread_ref[g], preferred_element_type=F32)
        y_parts, d_parts = [], []
        for pi in range(pairs):
            ha = hoff + g * (2 * pairs) + 2 * pi
            xp = u_ref[rs, g * GROUP_W + pi * LANES:g * GROUP_W + (pi + 1) * LANES]
            zero = jnp.zeros_like(xp)
            y_pair = None
            for hh, xm in ((ha, jnp.where(lane_lo, xp, zero)), (ha + 1, jnp.where(lane_lo, zero, xp))):
                diff = acs[:, hh:hh + 1] - acs_t[hh:hh + 1, :]
                dec = jnp.exp(jnp.where(keep, diff, -jnp.inf))
                m = (cb * dec * dt_t[hh:hh + 1, :]).astype(BF16)
                yh = jnp.dot(m, xm, preferred_element_type=F32)
                y_pair = yh if y_pair is None else y_pair + yh
            y_parts.append(y_pair)
            d_parts.append(_pair_cols(e_last, ha, lane_lo1))
        xw = (u_ref[rs, g * GROUP_W:(g + 1) * GROUP_W].astype(F32) * spread[:L]).astype(BF16)
        hs = ht_ref[g]
        y_off = jnp.dot(cg, hs.astype(BF16), preferred_element_type=F32) * spread[L:]
        s_new = jnp.dot(bg_t, xw, preferred_element_type=F32)
        ht_ref[g] = hs * jnp.concatenate(d_parts, axis=1) + s_new
        emit(g, jnp.concatenate(y_parts, axis=1) + y_off)


def _ssd_fwd_kernel(u_ref, dt_ref, cst_ref, spread_ref, o_ref, ht_ref):
    @pl.when(pl.program_id(0) == 0)
    def _():
        ht_ref[...] = jnp.zeros(ht_ref.shape, F32)

    for ci in range(SSD_STEP_CHUNKS):
        rs = slice(ci * SSM_CHUNK, (ci + 1) * SSM_CHUNK)

        def emit(g, y, rs=rs):
            o_ref[rs, g * GROUP_W:(g + 1) * GROUP_W] = y

        _ssd_chunk(u_ref, dt_ref, cst_ref, spread_ref, ht_ref, emit, rs, reverse=False)


def _ssd_bwd_kernel(u_ref, dt_ref, cst_ref, spread_ref, yf_ref, z0_ref, z1_ref, dsk_ref, nw_ref, o_ref, ht_ref, y_ref):
    @pl.when(pl.program_id(0) == 0)
    def _():
        ht_ref[...] = jnp.zeros(ht_ref.shape, F32)

    half = SSM_GROUPS // 2

    for ci in reversed(range(SSD_STEP_CHUNKS)):
        rs = slice(ci * SSM_CHUNK, (ci + 1) * SSM_CHUNK)

        def emit(g, y, rs=rs):
            sl = slice(g * GROUP_W, (g + 1) * GROUP_W)
            z_ref = z0_ref if g < half else z1_ref
            zsl = slice((g % half) * GROUP_W, (g % half + 1) * GROUP_W)
            tot = y + yf_ref[rs, sl] + dsk_ref[:, sl] * u_ref[rs, sl].astype(F32)
            y_ref[rs, sl] = tot * _silu(z_ref[rs, zsl].astype(F32))

        _ssd_chunk(u_ref, dt_ref, cst_ref, spread_ref, ht_ref, emit, rs, reverse=True)
    yv = y_ref[...]
    o_ref[...] = (yv * lax.rsqrt(jnp.mean(yv * yv, axis=-1, keepdims=True) + EPS) * nw_ref[...]).astype(o_ref.dtype)


def ssd_branch(u, dt_raw, cst, p, d_exp, ssm_norm_w, n_ctx_rows):
    r = u.shape[0]
    step_rows = SSD_STEP_CHUNKS * SSM_CHUNK
    n_chunks = r // step_rows
    n_ctx_chunks = n_ctx_rows // step_rows
    n_lat = n_chunks - n_ctx_chunks
    state = pltpu.VMEM((SSM_GROUPS, SSM_STATE, GROUP_W), F32)

    def fwd_idx(i):
        return jnp.where(i < n_ctx_chunks, n_lat + i, i - n_ctx_chunks)

    def bwd_idx(i):
        return n_chunks - 1 - i

    heads_per_group = SSM_HEADS // SSM_GROUPS
    hh = np.arange(LANES)[None, None, :, None]
    owner = (np.arange(2)[:, None, None, None] * SSM_HEADS + np.arange(SSM_GROUPS)[None, :, None, None] * heads_per_group
             + np.arange(GROUP_W)[None, None, None, :] // SSM_HEAD_DIM)
    spread = jnp.asarray(hh == owner, BF16)

    def spread_spec(direction):
        return pl.BlockSpec((None, SSM_GROUPS, LANES, GROUP_W), lambda i: (direction, 0, 0, 0))

    ublk = (step_rows, XBC_W)
    yf = pl.pallas_call(
        _ssd_fwd_kernel,
        grid=(n_chunks,),
        in_specs=[pl.BlockSpec(ublk, lambda i: (fwd_idx(i), 0)),
                  pl.BlockSpec((step_rows, LANES), lambda i: (fwd_idx(i), 0)),
                  pl.BlockSpec((8, LANES), lambda i: (0, 0)),
                  spread_spec(0)],
        out_specs=pl.BlockSpec((step_rows, BRANCH_W), lambda i: (fwd_idx(i), 0)),
        out_shape=jax.ShapeDtypeStruct((r, BRANCH_W), F32),
        scratch_shapes=[state],
        compiler_params=_params(("arbitrary",), 32),
        name="ssd_fwd",
    )(u, dt_raw, cst, spread)

    zw = BRANCH_W // 2
    z0 = OFF_Z // zw
    return pl.pallas_call(
        _ssd_bwd_kernel,
        grid=(n_chunks,),
        in_specs=[pl.BlockSpec(ublk, lambda i: (bwd_idx(i), 0)),
                  pl.BlockSpec((step_rows, LANES), lambda i: (bwd_idx(i), 0)),
                  pl.BlockSpec((8, LANES), lambda i: (0, 0)),
                  spread_spec(1),
                  pl.BlockSpec((step_rows, BRANCH_W), lambda i: (bwd_idx(i), 0)),
                  pl.BlockSpec((step_rows, zw), lambda i: (bwd_idx(i), z0)),
                  pl.BlockSpec((step_rows, zw), lambda i: (bwd_idx(i), z0 + 1)),
                  pl.BlockSpec((1, BRANCH_W), lambda i: (0, 0)),
                  pl.BlockSpec((1, BRANCH_W), lambda i: (0, 0))],
        out_specs=pl.BlockSpec((step_rows, BRANCH_W), lambda i: (bwd_idx(i), 0)),
        out_shape=jax.ShapeDtypeStruct((r, BRANCH_W), BF16),
        scratch_shapes=[state, pltpu.VMEM((step_rows, BRANCH_W), F32)],
        compiler_params=_params(("arbitrary",), 32),
        name="ssd_bwd",
    )(u, dt_raw, cst, spread, yf, p, p, d_exp, ssm_norm_w.reshape(1, BRANCH_W))


def _qkprep_kernel(qk_ref, cos_ref, sin_ref, qw_ref, kw_ref, swap_ref, q_ref, k_ref):
    cosv = cos_ref[...]
    sinv = sin_ref[...]
    scale = HEAD_DIM ** -0.5 * LOG2E
    swap = swap_ref[...]
    for h in range(ATT_HEADS + KV_HEADS):
        t = qk_ref[:, h * HEAD_DIM:(h + 1) * HEAD_DIM].astype(F32)
        w = qw_ref[...] if h < ATT_HEADS else kw_ref[...]
        n = t * lax.rsqrt(jnp.mean(t * t, axis=-1, keepdims=True) + EPS) * w
        partner = jnp.dot(n.astype(BF16), swap, preferred_element_type=F32)
        rot = n * cosv + partner * sinv
        if h < ATT_HEADS:
            q_ref[:, h * HEAD_DIM:(h + 1) * HEAD_DIM] = (rot * scale).astype(q_ref.dtype)
        else:
            hk = h - ATT_HEADS
            k_ref[:, hk * HEAD_DIM:(hk + 1) * HEAD_DIM] = rot.astype(k_ref.dtype)


def qk_prep(p, cos_t, sin_t, qw, kw):
    r = p.shape[0]
    tr = 256
    kvw = KV_HEADS * HEAD_DIM
    lane = np.arange(HEAD_DIM)
    swap = jnp.asarray(lane[:, None] == (lane ^ 1)[None, :], BF16)
    return pl.pallas_call(
        _qkprep_kernel,
        grid=(r // tr,),
        in_specs=[pl.BlockSpec((tr, QK_W), lambda i: (i, OFF_QA // QK_W)),
                  pl.BlockSpec((tr, HEAD_DIM), lambda i: (i, 0)),
                  pl.BlockSpec((tr, HEAD_DIM), lambda i: (i, 0)),
                  pl.BlockSpec((1, HEAD_DIM), lambda i: (0, 0)),
                  pl.BlockSpec((1, HEAD_DIM), lambda i: (0, 0)),
                  pl.BlockSpec((HEAD_DIM, HEAD_DIM), lambda i: (0, 0))],
        out_specs=[pl.BlockSpec((tr, BRANCH_W), lambda i: (i, 0)),
                   pl.BlockSpec((tr, kvw), lambda i: (i, 0))],
        out_shape=[jax.ShapeDtypeStruct((r, BRANCH_W), BF16),
                   jax.ShapeDtypeStruct((r, kvw), BF16)],
        compiler_params=_params(("arbitrary",), 32),
        name="qk_prep",
    )(p, cos_t, sin_t, qw, kw, swap)


def _flash_kernel(q_ref, k_ref, v_ref, g_ref, o_ref, m_ref, l_ref, acc_ref, *, nk):
    ki = pl.program_id(2)

    @pl.when(ki == 0)
    def _():
        m_ref[...] = jnp.full(m_ref.shape, -jnp.inf, F32)
        l_ref[...] = jnp.zeros(l_ref.shape, F32)
        acc_ref[...] = jnp.zeros(acc_ref.shape, F32)

    k = k_ref[...]
    v1 = jnp.concatenate([v_ref[...], jnp.ones(v_ref.shape, v_ref.dtype)], axis=1)
    reps = k.shape[0] // LANES
    for r in range(ATT_GROUP):
        hs = slice(r * HEAD_DIM, (r + 1) * HEAD_DIM)
        s = lax.dot_general(q_ref[:, hs], k, NT_DIMS, preferred_element_type=F32)
        m_prev = m_ref[r]
        m_next = jnp.maximum(m_prev, jnp.max(s, axis=1, keepdims=True))
        p = jnp.exp2(s - jnp.concatenate([m_next] * reps, axis=1))
        alpha = jnp.exp2(m_prev - m_next)
        pv = jnp.dot(p.astype(v1.dtype), v1, preferred_element_type=F32)
        l_ref[r] = alpha * l_ref[r] + pv[:, HEAD_DIM:]
        acc_ref[r] = alpha * acc_ref[r] + pv[:, :HEAD_DIM]
        m_ref[r] = m_next

    @pl.when(ki == nk - 1)
    def _():
        for r in range(ATT_GROUP):
            hs = slice(r * HEAD_DIM, (r + 1) * HEAD_DIM)
            o = acc_ref[r] / l_ref[r]
            o_ref[:, hs] = (_silu(g_ref[:, hs].astype(F32)) * o).astype(o_ref.dtype)


def flash_gqa(qn, kn, p, *, q_rows, q_row0, k_rows, k_row0, tq, tk):
    nq = q_rows // tq
    nk = k_rows // tk
    qb0 = q_row0 // tq
    kb0 = k_row0 // tk
    gw = ATT_GROUP * HEAD_DIM
    return pl.pallas_call(
        functools.partial(_flash_kernel, nk=nk),
        grid=(KV_HEADS, nq, nk),
        in_specs=[pl.BlockSpec((tq, gw), lambda g, i, j: (qb0 + i, g)),
                  pl.BlockSpec((tk, HEAD_DIM), lambda g, i, j: (kb0 + j, g)),
                  pl.BlockSpec((tk, HEAD_DIM), lambda g, i, j: (kb0 + j, OFF_VA // HEAD_DIM + g)),
                  pl.BlockSpec((tq, gw), lambda g, i, j: (qb0 + i, OFF_GA // gw + g))],
        out_specs=pl.BlockSpec((tq, gw), lambda g, i, j: (i, g)),
        out_shape=jax.ShapeDtypeStruct((q_rows, BRANCH_W), BF16),
        scratch_shapes=[pltpu.VMEM((ATT_GROUP, tq, LANES), F32),
                        pltpu.VMEM((ATT_GROUP, tq, LANES), F32),
                        pltpu.VMEM((ATT_GROUP, tq, HEAD_DIM), F32)],
        compiler_params=_params(("arbitrary", "arbitrary", "arbitrary"), 48),
        name="flash_gqa",
    )(qn, kn, p, p)


NA_QROWS = 4
NA_KROWS = 12
NA_PATTERNS = 3


def _na_rel_row(pattern, qr, kr):
    if pattern == 0:
        return kr - qr + (NA_WIN_R - 1) if kr < NA_WIN_R else None
    if pattern == 1:
        return kr - qr + (NA_WIN_R - 1 - NA_WIN_R // 2) if 0 <= kr - qr < NA_WIN_R else None
    first = NA_KROWS - NA_WIN_R
    return kr - qr + (NA_WIN_R - 1) - (NA_KROWS - NA_QROWS) if kr >= first else None


def _na_kernel(q_ref, k_ref, v_ref, g_ref, tp_ref, o_ref, bias_ref, qs_ref, v1_ref, *, n_lat, n_ctx, grid_rows,
               prep_rows):
    qn = NA_QROWS * GRID_W
    win = NA_KROWS * GRID_W
    n_blocks = grid_rows // NA_QROWS
    rows = n_lat + n_ctx
    for r0 in range(0, rows, prep_rows):
        rs = slice(r0, r0 + prep_rows)
        qs_ref[rs, :] = (q_ref[rs, :].astype(F32) * (HEAD_DIM ** -0.5 * LOG2E)).astype(BF16)
        v1_ref[rs, 0:HEAD_DIM] = v_ref[rs, :]
        v1_ref[rs, HEAD_DIM:2 * HEAD_DIM] = jnp.ones((prep_rows, HEAD_DIM), BF16)
    kc = k_ref[n_lat:rows, :]
    vc = v1_ref[n_lat:rows, :]

    lane_lo = lax.broadcasted_iota(jnp.int32, (GRID_W, LANES), 1) < GRID_W
    for pattern in range(NA_PATTERNS):
        for qr in range(NA_QROWS):
            for kp in range(NA_KROWS // 2):
                ja = _na_rel_row(pattern, qr, 2 * kp)
                jb = _na_rel_row(pattern, qr, 2 * kp + 1)
                if ja is None and jb is None:
                    tile = jnp.full((GRID_W, LANES), NEG_BIAS, F32)
                elif jb is None:
                    tile = jnp.where(lane_lo, tp_ref[ja + 1], NEG_BIAS)
                elif ja is None:
                    tile = jnp.where(lane_lo, NEG_BIAS, tp_ref[jb])
                else:
                    tile = tp_ref[jb]
                bias_ref[pattern, qr * GRID_W:(qr + 1) * GRID_W, kp * LANES:(kp + 1) * LANES] = tile

    def block_body(b, carry):
        u0 = jnp.clip(b * NA_QROWS - NA_WIN_R // 2, 0, grid_rows - NA_KROWS)
        pattern = jnp.where(b == 0, 0, jnp.where(b == n_blocks - 1, 2, 1))
        qs = pl.ds(pl.multiple_of(b * qn, qn), qn)
        ws = pl.ds(pl.multiple_of(u0 * GRID_W, GRID_W), win)
        q = qs_ref[qs, :]
        s_loc = lax.dot_general(q, k_ref[ws, :], NT_DIMS, preferred_element_type=F32) + bias_ref[pattern]
        s_ctx = lax.dot_general(q, kc, NT_DIMS, preferred_element_type=F32)
        m = jnp.maximum(jnp.max(s_loc, axis=1, keepdims=True), jnp.max(s_ctx, axis=1, keepdims=True))
        pv = (jnp.dot(jnp.exp2(s_loc - m).astype(BF16), v1_ref[ws, :], preferred_element_type=F32)
              + jnp.dot(jnp.exp2(s_ctx - m).astype(BF16), vc, preferred_element_type=F32))
        o = pv[:, :HEAD_DIM] / pv[:, HEAD_DIM:]
        o_ref[qs, :] = (_silu(g_ref[qs, :].astype(F32)) * o).astype(o_ref.dtype)
        return carry

    lax.fori_loop(0, n_blocks, block_body, 0, unroll=16)

    qs = slice(n_lat, rows)
    s = lax.dot_general(qs_ref[qs, :], kc, NT_DIMS, preferred_element_type=F32)
    pvc = jnp.dot(jnp.exp2(s - jnp.max(s, axis=1, keepdims=True)).astype(BF16), vc, preferred_element_type=F32)
    oc = pvc[:, :HEAD_DIM] / pvc[:, HEAD_DIM:]
    o_ref[qs, :] = (_silu(g_ref[qs, :].astype(F32)) * oc).astype(o_ref.dtype)


def neighbourhood(p, tp, n_lat):
    r = p.shape[0]
    n_ctx = r - n_lat
    col = lambda off: (lambda h: (0, off // HEAD_DIM + h))
    return pl.pallas_call(
        functools.partial(_na_kernel, n_lat=n_lat, n_ctx=n_ctx, grid_rows=n_lat // GRID_W,
                          prep_rows=_pick(r, (1056, 640, 256, 128))),
        grid=(NA_HEADS,),
        in_specs=[pl.BlockSpec((r, HEAD_DIM), col(OFF_QN)),
                  pl.BlockSpec((r, HEAD_DIM), col(OFF_KN)),
                  pl.BlockSpec((r, HEAD_DIM), col(OFF_VN)),
                  pl.BlockSpec((r, HEAD_DIM), col(OFF_GN)),
                  pl.BlockSpec((None, 2 * NA_WIN_R, GRID_W, 2 * GRID_W), lambda h: (h, 0, 0, 0))],
        out_specs=pl.BlockSpec((r, HEAD_DIM), lambda h: (0, h)),
        out_shape=jax.ShapeDtypeStruct((r, BRANCH_W), BF16),
        scratch_shapes=[pltpu.VMEM((NA_PATTERNS, NA_QROWS * GRID_W, NA_KROWS * GRID_W), F32),
                        pltpu.VMEM((r, HEAD_DIM), BF16),
                        pltpu.VMEM((r, 2 * HEAD_DIM), BF16)],
        compiler_params=_params(("arbitrary",), 48),
        name="neighbourhood",
    )(p, p, p, p, tp)


def na_bias_table(rpb):
    n_rel = 2 * NA_WIN_C - 1
    c = np.arange(GRID_W)[:, None]
    kc = np.arange(GRID_W)[None, :]
    c0 = np.clip(c - NA_WIN_C // 2, 0, GRID_W - NA_WIN_C)
    valid = (kc >= c0) & (kc < c0 + NA_WIN_C)
    rel = kc - c + (NA_WIN_C - 1)
    onehot = (rel[None] == np.arange(n_rel)[:, None, None]) & valid[None]
    h, nr, _ = rpb.shape
    t = jnp.dot(rpb.astype(F32).reshape(h * nr, n_rel), jnp.asarray(onehot.reshape(n_rel, -1), F32),
                precision=lax.Precision.HIGHEST).reshape(h, nr, GRID_W, GRID_W)
    t = jnp.where(jnp.asarray(valid), t * LOG2E, NEG_BIAS)
    masked = jnp.full((h, 1, GRID_W, GRID_W), NEG_BIAS, F32)
    tpad = jnp.concatenate([masked, t, masked], axis=1)
    return jnp.concatenate([tpad[:, :-1], tpad[:, 1:]], axis=-1)


def _gateup_kernel(a_ref, b_ref, c_ref, wa_ref, wb_ref, wc_ref, ga_ref, gb_ref, gc_ref, o_ref):
    acc = None
    for br_ref, w_ref, g_ref in ((a_ref, wa_ref, ga_ref), (b_ref, wb_ref, gb_ref), (c_ref, wc_ref, gc_ref)):
        up = jnp.dot(br_ref[...], w_ref[...], preferred_element_type=F32)
        term = _sigmoid(g_ref[...].astype(F32)) * up
        acc = term if acc is None else acc + term
    o_ref[...] = acc.astype(o_ref.dtype)


def gate_up(br_a, br_b, br_c, w_up, p, layer, *, rows, row0s, p_row0):
    tm = _pick(rows, (1024, 512, 256))
    tn = 512

    def br_spec(row0):
        return pl.BlockSpec((tm, BRANCH_W), lambda i, j: (row0 // tm + i, 0))

    def w_spec(b):
        return pl.BlockSpec((None, None, BRANCH_W, tn), lambda i, j: (layer, b, 0, j))

    def g_spec(b):
        return pl.BlockSpec((tm, tn), lambda i, j: (p_row0 // tm + i, (OFF_GM + b * D_MODEL) // tn + j))

    return pl.pallas_call(
        _gateup_kernel,
        grid=(rows // tm, D_MODEL // tn),
        in_specs=[br_spec(row0s[0]), br_spec(row0s[1]), br_spec(row0s[2]),
                  w_spec(0), w_spec(1), w_spec(2), g_spec(0), g_spec(1), g_spec(2)],
        out_specs=pl.BlockSpec((tm, tn), lambda i, j: (i, j)),
        out_shape=jax.ShapeDtypeStruct((rows, D_MODEL), BF16),
        compiler_params=_params(("arbitrary", "arbitrary"), 56),
        name="gate_up",
    )(br_a, br_b, br_c, w_up, w_up, w_up, p, p, p)


def _outproj_kernel(s_ref, w_ref, x_ref, mod_ref, o_ref, *, mod_row):
    y = jnp.dot(s_ref[...], w_ref[...], preferred_element_type=F32)
    o_ref[...] = x_ref[...] + mod_ref[mod_row:mod_row + 1, :] * y


def out_proj(s, w_out, x, mod, layer, mod_row):
    rows = s.shape[0]
    tm = _pick(rows, (1024, 512, 256))
    tn = 512
    return pl.pallas_call(
        functools.partial(_outproj_kernel, mod_row=mod_row),
        grid=(rows // tm, D_MODEL // tn),
        in_specs=[pl.BlockSpec((tm, D_MODEL), lambda i, j: (i, 0)),
                  pl.BlockSpec((None, D_MODEL, tn), lambda i, j: (layer, 0, j)),
                  pl.BlockSpec((tm, tn), lambda i, j: (i, j)),
                  pl.BlockSpec((8, tn), lambda i, j: (0, 2 * D_MODEL // tn + j))],
        out_specs=pl.BlockSpec((tm, tn), lambda i, j: (i, j)),
        out_shape=jax.ShapeDtypeStruct((rows, D_MODEL), F32),
        compiler_params=_params(("arbitrary", "arbitrary"), 48),
        name="out_proj",
    )(s, w_out, x, mod)


def _final_norm_kernel(x_ref, w_ref, o_ref):
    xv = x_ref[...]
    o_ref[...] = xv * lax.rsqrt(jnp.mean(xv * xv, axis=-1, keepdims=True) + EPS) * w_ref[...]


def final_norm(x, w):
    t, d = x.shape
    tr = 256
    return pl.pallas_call(
        _final_norm_kernel,
        grid=(t // tr,),
        in_specs=[pl.BlockSpec((tr, d), lambda i: (i, 0)), pl.BlockSpec((1, d), lambda i: (0, 0))],
        out_specs=pl.BlockSpec((tr, d), lambda i: (i, 0)),
        out_shape=jax.ShapeDtypeStruct((t, d), F32),
        compiler_params=_params(("arbitrary",), 40),
        name="final_norm",
    )(x, w.reshape(1, d))


def _rope_tables(n_lat, n_ctx):
    pos = jnp.arange(n_lat, dtype=jnp.int32)
    row = (pos // GRID_W).astype(F32)
    colp = (pos % GRID_W).astype(F32)
    n_freq = HEAD_DIM // 4
    inv_freq = ROPE_THETA ** (-jnp.arange(n_freq, dtype=F32) / n_freq)
    ang = jnp.concatenate([row[:, None] * inv_freq, colp[:, None] * inv_freq], axis=-1)
    cos, sin = jnp.cos(ang), jnp.sin(ang)
    cos_t = jnp.repeat(cos, 2, axis=-1)
    sin_t = jnp.stack([-sin, sin], axis=-1).reshape(n_lat, HEAD_DIM)
    cos_t = jnp.concatenate([cos_t, jnp.ones((n_ctx, HEAD_DIM), F32)], axis=0)
    sin_t = jnp.concatenate([sin_t, jnp.zeros((n_ctx, HEAD_DIM), F32)], axis=0)
    return cos_t, sin_t


def hybrid_layer(x, ctx, sb, lp, cos_t, sin_t, update_ctx):
    t = x.shape[0]
    lc = ctx.shape[0]
    layer = lp["layer"]
    mod = modulation(sb, lp["w_mod"], lp["b_mod"], layer)
    h = norm_modulate(x, ctx, lp["norm_w"], mod)
    p = in_proj(h, lp["w_in_t"], lp["b_main"], layer)
    dt_raw = dt_proj(h, lp["w_in_t"], lp["b_dt"], layer)

    u = ssd_conv(p, lp["conv_w"], lp["conv_b"], t // SSM_CHUNK)
    br_a = ssd_branch(u, dt_raw, lp["ssd_cst"], p, lp["d_exp"], lp["ssm_norm_w"], lc)

    qn, kn = qk_prep(p, cos_t, sin_t, lp["q_norm_w"], lp["k_norm_w"])
    tq = _pick(t, (1024, 512, 256))
    tk = _pick(t + lc, (1408, 768, 640, 512, 384, 256))
    br_b = flash_gqa(qn, kn, p, q_rows=t, q_row0=0, k_rows=t + lc, k_row0=0, tq=tq, tk=tk)

    br_c = neighbourhood(p, lp["na_bias"], t)

    s = gate_up(br_a, br_b, br_c, lp["w_up"], p, layer, rows=t, row0s=(0, 0, 0), p_row0=0)
    x_new = out_proj(s, lp["w_out"], x, mod, layer, 0)
    if not update_ctx:
        return x_new, ctx
    br_b_c = flash_gqa(qn, kn, p, q_rows=lc, q_row0=t, k_rows=lc, k_row0=t, tq=lc, tk=lc)
    s_c = gate_up(br_a, br_b_c, br_c, lp["w_up"], p, layer, rows=lc, row0s=(t, 0, t), p_row0=t)
    ctx_new = out_proj(s_c, lp["w_out"], ctx, mod, layer, 1)
    return x_new, ctx_new


def layer_params(layer, w_mod, b_mod, norm_w, w_in_t, b_in, conv_w, conv_b, a_log, dt_bias, d_skip, ssm_norm_w,
                 q_norm_w, k_norm_w, rpb, w_up, w_out):
    pad_heads = [(0, 0), (0, LANES - DT_W)]
    b = b_in[layer].astype(F32)
    cst = jnp.concatenate([
        jnp.pad(dt_bias[layer].astype(F32).reshape(1, DT_W), pad_heads),
        jnp.pad(-jnp.exp(a_log[layer].astype(F32)).reshape(1, DT_W), pad_heads),
        jnp.zeros((6, LANES), F32)], axis=0)
    return dict(
        layer=layer, w_mod=w_mod, b_mod=b_mod[layer], norm_w=norm_w[layer], w_in_t=w_in_t,
        b_main=jnp.concatenate([b[:_SRC_DT], b[_SRC_QA:]]),
        b_dt=jnp.pad(b[_SRC_DT:_SRC_QA], (0, LANES - DT_W)),
        conv_w=conv_w[layer], conv_b=conv_b[layer], ssd_cst=cst,
        d_exp=jnp.repeat(d_skip[layer].astype(F32), SSM_HEAD_DIM).reshape(1, BRANCH_W),
        ssm_norm_w=ssm_norm_w[layer],
        q_norm_w=q_norm_w[layer].reshape(1, HEAD_DIM), k_norm_w=k_norm_w[layer].reshape(1, HEAD_DIM),
        na_bias=na_bias_table(rpb[layer]), w_up=w_up, w_out=w_out)


def kernel(x, c, ctx, c_ctx, norm_w, w_mod, b_mod, w_in, b_in, conv_w, conv_b, a_log, dt_bias, d_skip, ssm_norm_w, q_norm_w, k_norm_w, rpb, w_up, w_out, final_norm_w):
    xs = x[0]
    cs = ctx[0]
    sb = jnp.broadcast_to(jnp.stack([c[0], c_ctx])[:, :, None], (2, D_MODEL, LANES))
    cos_t, sin_t = _rope_tables(xs.shape[0], cs.shape[0])
    w_in_t = jnp.swapaxes(w_in, 1, 2)
    w_up_b = w_up.astype(BF16)
    w_out_b = w_out.astype(BF16)
    for layer in range(DEPTH):
        lp = layer_params(layer, w_mod, b_mod, norm_w, w_in_t, b_in, conv_w, conv_b, a_log, dt_bias, d_skip,
                          ssm_norm_w, q_norm_w, k_norm_w, rpb, w_up_b, w_out_b)
        xs, cs = hybrid_layer(xs, cs, sb, lp, cos_t, sin_t, layer < DEPTH - 1)
    return final_norm(xs, final_norm_w)[None]
```

```python
import functools
import math

import numpy as np
import jax
import jax.numpy as jnp
from jax import lax
from jax.experimental import pallas as pl
from jax.experimental.pallas import tpu as pltpu

F32 = jnp.float32
BF16 = jnp.bfloat16

D_MODEL = 4096
DEPTH = 2
GRID_W = 64
N_BRANCH = 3
BRANCH_W = 2048
EPS = 1e-6

SSM_HEADS = 32
SSM_HEAD_DIM = 64
SSM_GROUPS = 4
SSM_STATE = 128
SSM_CHUNK = 128
CONV_K = 5
SSM_GN = SSM_GROUPS * SSM_STATE
XBC_W = BRANCH_W + 2 * SSM_GN
DT_W = 2 * SSM_HEADS
GROUP_W = (SSM_HEADS // SSM_GROUPS) * SSM_HEAD_DIM

ATT_HEADS = 16
KV_HEADS = 4
ATT_GROUP = ATT_HEADS // KV_HEADS
HEAD_DIM = 128
ROPE_THETA = 10000.0

NA_HEADS = 16
NA_WIN_R = 8
NA_WIN_C = 16

LANES = 128
MIB = 1024 * 1024

_SRC_DT = XBC_W + BRANCH_W
_SRC_QA = _SRC_DT + 2 * SSM_HEADS
OFF_XBC = 0
OFF_Z = OFF_XBC + XBC_W
OFF_QA = OFF_Z + BRANCH_W
OFF_KA = OFF_QA + BRANCH_W
OFF_VA = OFF_KA + KV_HEADS * HEAD_DIM
OFF_GA = OFF_VA + KV_HEADS * HEAD_DIM
OFF_QN = OFF_GA + BRANCH_W
OFF_KN = OFF_QN + BRANCH_W
OFF_VN = OFF_KN + BRANCH_W
OFF_GN = OFF_VN + BRANCH_W
OFF_GM = OFF_GN + BRANCH_W
P_COLS = OFF_GM + N_BRANCH * D_MODEL
QK_W = BRANCH_W + KV_HEADS * HEAD_DIM

NEG_BIAS = -1e30
LOG2E = math.log2(math.e)


def _params(sem, vmem_mib):
    return pltpu.CompilerParams(dimension_semantics=sem, vmem_limit_bytes=vmem_mib * MIB)


def _pick(n, candidates):
    for c in candidates:
        if n % c == 0:
            return c
    raise ValueError(f"no tile for {n} in {candidates}")


def _sigmoid(v):
    return 1.0 / (1.0 + jnp.exp(-v))


def _silu(v):
    return v * _sigmoid(v)


def _mod_kernel(s_ref, w_ref, b_ref, o_ref, act_ref):
    tn = w_ref.shape[1]
    k = w_ref.shape[0]

    @pl.when(pl.program_id(0) == 0)
    def _():
        act_ref[...] = _silu(s_ref[...])

    o_ref[...] = jnp.zeros(o_ref.shape, F32)
    for v in range(2):
        s = act_ref[v]
        for jb in range(tn // LANES):
            sl = slice(jb * LANES, (jb + 1) * LANES)
            prod = w_ref[:, sl] * s
            part = prod.reshape(k // 8, 8, LANES).sum(axis=0)
            o_ref[v:v + 1, sl] = part.sum(axis=0, keepdims=True) + b_ref[:, sl]


def modulation(sb, w_mod, b_mod, layer):
    _, k, n = w_mod.shape
    tn = 512
    return pl.pallas_call(
        _mod_kernel,
        grid=(n // tn,),
        in_specs=[pl.BlockSpec((2, k, LANES), lambda j: (0, 0, 0)),
                  pl.BlockSpec((None, k, tn), lambda j: (layer, 0, j)),
                  pl.BlockSpec((1, tn), lambda j: (0, j))],
        out_specs=pl.BlockSpec((8, tn), lambda j: (0, j)),
        out_shape=jax.ShapeDtypeStruct((8, n), F32),
        scratch_shapes=[pltpu.VMEM((2, k, LANES), F32)],
        compiler_params=_params(("arbitrary",), 40),
        name="modulation",
    )(sb, w_mod, b_mod.reshape(1, n))


def _norm_kernel(x_ref, c_ref, nw_ref, mod_ref, o_ref, *, n_lat_blocks):
    i = pl.program_id(0)

    def emit(src_ref, row):
        xv = src_ref[...]
        y = xv * lax.rsqrt(jnp.mean(xv * xv, axis=-1, keepdims=True) + EPS) * nw_ref[...]
        shift = mod_ref[row:row + 1, 0:D_MODEL]
        scale = mod_ref[row:row + 1, D_MODEL:2 * D_MODEL]
        o_ref[...] = (y * (1.0 + scale) + shift).astype(o_ref.dtype)

    @pl.when(i < n_lat_blocks)
    def _():
        emit(x_ref, 0)

    @pl.when(i >= n_lat_blocks)
    def _():
        emit(c_ref, 1)


def norm_modulate(x, ctx, norm_w, mod):
    t, d = x.shape
    lc = ctx.shape[0]
    tr = 256
    n_lat = t // tr
    n_ctx = lc // tr
    return pl.pallas_call(
        functools.partial(_norm_kernel, n_lat_blocks=n_lat),
        grid=(n_lat + n_ctx,),
        in_specs=[pl.BlockSpec((tr, d), lambda i: (jnp.minimum(i, n_lat - 1), 0)),
                  pl.BlockSpec((tr, d), lambda i: (jnp.maximum(i - n_lat, 0), 0)),
                  pl.BlockSpec((1, d), lambda i: (0, 0)),
                  pl.BlockSpec((8, 3 * d), lambda i: (0, 0))],
        out_specs=pl.BlockSpec((tr, d), lambda i: (i, 0)),
        out_shape=jax.ShapeDtypeStruct((t + lc, d), BF16),
        compiler_params=_params(("arbitrary",), 40),
        name="norm_modulate",
    )(x, ctx, norm_w.reshape(1, d), mod)


NT_DIMS = (((1,), (1,)), ((), ()))


def _inproj_kernel(h_ref, wa_ref, wb_ref, b_ref, o_ref, *, n_plain):
    def emit(w):
        acc = lax.dot_general(h_ref[...], w.astype(BF16), NT_DIMS, preferred_element_type=F32)
        o_ref[...] = (acc + b_ref[...]).astype(o_ref.dtype)

    j = pl.program_id(1)

    @pl.when(j < n_plain)
    def _():
        emit(wa_ref[...])

    @pl.when(j >= n_plain)
    def _():
        emit(jnp.concatenate([wa_ref[DT_W:, :], wb_ref[...]], axis=0))


def in_proj(h, w_t, b_main, layer):
    m, k = h.shape
    tm = _pick(m, (2112, 1280, 1024, 768, 640, 512, 256, 128))
    tn = 512
    return pl.pallas_call(
        functools.partial(_inproj_kernel, n_plain=_SRC_DT // tn),
        grid=(m // tm, P_COLS // tn),
        in_specs=[pl.BlockSpec((tm, k), lambda i, j: (i, 0), pipeline_mode=pl.Buffered(1)),
                  pl.BlockSpec((None, tn, k), lambda i, j: (layer, j, 0)),
                  pl.BlockSpec((None, DT_W, k), lambda i, j: (layer, (j + 1) * (tn // DT_W), 0)),
                  pl.BlockSpec((1, tn), lambda i, j: (0, j))],
        out_specs=pl.BlockSpec((tm, tn), lambda i, j: (i, j)),
        out_shape=jax.ShapeDtypeStruct((m, P_COLS), BF16),
        compiler_params=_params(("arbitrary", "arbitrary"), 52),
        name="in_proj",
    )(h, w_t, w_t, b_main.reshape(1, P_COLS))


def _dtproj_kernel(h_ref, w_ref, b_ref, o_ref):
    acc = lax.dot_general(h_ref[...], w_ref[...].astype(BF16), NT_DIMS, preferred_element_type=F32)
    o_ref[...] = acc + b_ref[...]


def dt_proj(h, w_t, b_dt, layer):
    m, k = h.shape
    tm = _pick(m, (1408, 1280, 1024, 768, 640, 512, 256, 128))
    return pl.pallas_call(
        _dtproj_kernel,
        grid=(m // tm,),
        in_specs=[pl.BlockSpec((tm, k), lambda i: (i, 0)),
                  pl.BlockSpec((None, LANES, k), lambda i: (layer, _SRC_DT // LANES, 0)),
                  pl.BlockSpec((1, LANES), lambda i: (0, 0))],
        out_specs=pl.BlockSpec((tm, LANES), lambda i: (i, 0)),
        out_shape=jax.ShapeDtypeStruct((m, LANES), F32),
        compiler_params=_params(("arbitrary",), 48),
        name="dt_proj",
    )(h, w_t, b_dt.reshape(1, LANES))


def _conv_kernel(prev_ref, cur_ref, next_ref, w_ref, b_ref, shift_ref, o_ref, *, n_lat_chunks, n_chunks):
    c = pl.program_id(0)
    has_prev = jnp.logical_and(c != 0, c != n_lat_chunks)
    has_next = jnp.logical_and(c != n_lat_chunks - 1, c != n_chunks - 1)
    pad = CONV_K // 2
    cw = 512
    shift = shift_ref[...]
    for jb in range(XBC_W // cw):
        sl = slice(jb * cw, (jb + 1) * cw)
        cur = cur_ref[:, sl].astype(F32)
        pv = prev_ref[SSM_CHUNK - 16:SSM_CHUNK, sl].astype(F32)[8:16]
        nx = next_ref[0:16, sl].astype(F32)[0:8]
        pv = jnp.where(has_prev, pv, 0.0)
        nx = jnp.where(has_next, nx, 0.0)
        ext = jnp.concatenate([pv, cur, nx], axis=0)
        taps = jnp.dot(shift, ext.astype(BF16), preferred_element_type=F32)
        acc = b_ref[:, sl] + w_ref[pad:pad + 1, sl] * cur
        j = 0
        for kk in range(CONV_K):
            if kk != pad:
                acc = acc + w_ref[kk:kk + 1, sl] * taps[j * SSM_CHUNK:(j + 1) * SSM_CHUNK]
                j += 1
        o_ref[:, sl] = _silu(acc).astype(o_ref.dtype)


def ssd_conv(p, conv_w, conv_b, n_lat_chunks):
    r = p.shape[0]
    n_chunks = r // SSM_CHUNK

    def prev_map(c):
        bad = jnp.logical_or(c == 0, c == n_lat_chunks)
        return (jnp.where(bad, c, c - 1), 0)

    def next_map(c):
        bad = jnp.logical_or(c == n_lat_chunks - 1, c == n_chunks - 1)
        return (jnp.where(bad, c, c + 1), 0)

    blk = (SSM_CHUNK, XBC_W)
    n_ext = SSM_CHUNK + 16
    offs = [kk - CONV_K // 2 for kk in range(CONV_K) if kk != CONV_K // 2]
    src = np.concatenate([np.arange(SSM_CHUNK) + 8 + o for o in offs])
    shift = jnp.asarray(src[:, None] == np.arange(n_ext)[None, :], BF16)
    return pl.pallas_call(
        functools.partial(_conv_kernel, n_lat_chunks=n_lat_chunks, n_chunks=n_chunks),
        grid=(n_chunks,),
        in_specs=[pl.BlockSpec(blk, prev_map),
                  pl.BlockSpec(blk, lambda c: (c, 0)),
                  pl.BlockSpec(blk, next_map),
                  pl.BlockSpec((8, XBC_W), lambda c: (0, 0)),
                  pl.BlockSpec((1, XBC_W), lambda c: (0, 0)),
                  pl.BlockSpec((len(offs) * SSM_CHUNK, n_ext), lambda c: (0, 0))],
        out_specs=pl.BlockSpec(blk, lambda c: (c, 0)),
        out_shape=jax.ShapeDtypeStruct((r, XBC_W), BF16),
        compiler_params=_params(("arbitrary",), 32),
        name="ssd_conv",
    )(p, p, p, jnp.pad(conv_w, ((0, 8 - CONV_K), (0, 0))), conv_b.reshape(1, XBC_W), shift)


def _dot3(tri, v):
    hi = v.astype(BF16)
    r1 = v - hi.astype(F32)
    mid = r1.astype(BF16)
    lo = (r1 - mid.astype(F32)).astype(BF16)
    return (jnp.dot(tri, hi, preferred_element_type=F32)
            + jnp.dot(tri, mid, preferred_element_type=F32)
            + jnp.dot(tri, lo, preferred_element_type=F32))


def _pair_cols(v, ha, lane_lo):
    rows = v.shape[0]
    a = jnp.broadcast_to(v[:, ha:ha + 1], (rows, LANES))
    b = jnp.broadcast_to(v[:, ha + 1:ha + 2], (rows, LANES))
    return jnp.where(lane_lo, a, b)


SSD_STEP_CHUNKS = 2


def _ssd_chunk(u_ref, dt_ref, cst_ref, spread_ref, ht_ref, emit, rs, *, reverse):
    L = SSM_CHUNK
    hoff = SSM_HEADS if reverse else 0
    li = lax.broadcasted_iota(jnp.int32, (L, L), 0)
    si = lax.broadcasted_iota(jnp.int32, (L, L), 1)
    keep = (li <= si) if reverse else (li >= si)
    tri = jnp.where((si >= li) if reverse else (si <= li), 1.0, 0.0).astype(BF16)
    lane_lo = lax.broadcasted_iota(jnp.int32, (L, LANES), 1) < SSM_HEAD_DIM
    lane_lo1 = lax.broadcasted_iota(jnp.int32, (1, LANES), 1) < SSM_HEAD_DIM

    raw = dt_ref[rs, :] + cst_ref[0:1, :]
    dtv = jnp.maximum(raw, 0.0) + jnp.log1p(jnp.exp(-jnp.abs(raw)))
    da = dtv * cst_ref[1:2, :]
    acs = _dot3(tri, da)
    acs_t = acs.T
    dt_t = dtv.T
    a_last = acs[0:1, :] if reverse else acs[L - 1:L, :]
    e_last = jnp.exp(a_last)
    per_head = jnp.concatenate([jnp.exp(a_last - acs) * dtv, jnp.exp(acs)], axis=0).astype(BF16)

    pairs = GROUP_W // LANES
    for g in range(SSM_GROUPS):
        bg = u_ref[rs, BRANCH_W + g * SSM_STATE:BRANCH_W + (g + 1) * SSM_STATE]
        cg = u_ref[rs, BRANCH_W + SSM_GN + g * SSM_STATE:BRANCH_W + SSM_GN + (g + 1) * SSM_STATE]
        cb = lax.dot_general(cg, bg, NT_DIMS, preferred_element_type=F32)
        bg_t = bg.astype(F32).T.astype(BF16)
        spread = jnp.dot(per_head, spread_ref[g], preferred_element_type=F32)
        y_parts, d_parts = [], []
        for pi in range(pairs):
            ha = hoff + g * (2 * pairs) + 2 * pi
            xp = u_ref[rs, g * GROUP_W + pi * LANES:g * GROUP_W + (pi + 1) * LANES]
            zero = jnp.zeros_like(xp)
            y_pair = None
            for hh, xm in ((ha, jnp.where(lane_lo, xp, zero)), (ha + 1, jnp.where(lane_lo, zero, xp))):
                diff = acs[:, hh:hh + 1] - acs_t[hh:hh + 1, :]
                dec = jnp.exp(jnp.where(keep, diff, -jnp.inf))
                m = (cb * dec * dt_t[hh:hh + 1, :]).astype(BF16)
                yh = jnp.dot(m, xm, preferred_element_type=F32)
                y_pair = yh if y_pair is None else y_pair + yh
            y_parts.append(y_pair)
            d_parts.append(_pair_cols(e_last, ha, lane_lo1))
        xw = (u_ref[rs, g * GROUP_W:(g + 1) * GROUP_W].astype(F32) * spread[:L]).astype(BF16)
        hs = ht_ref[g]
        y_off = jnp.dot(cg, hs.astype(BF16), preferred_element_type=F32) * spread[L:]
        s_new = jnp.dot(bg_t, xw, preferred_element_type=F32)
        ht_ref[g] = hs * jnp.concatenate(d_parts, axis=1) + s_new
        emit(g, jnp.concatenate(y_parts, axis=1) + y_off)


def _ssd_fwd_kernel(u_ref, dt_ref, cst_ref, spread_ref, o_ref, ht_ref):
    @pl.when(pl.program_id(0) == 0)
    def _():
        ht_ref[...] = jnp.zeros(ht_ref.shape, F32)

    for ci in range(SSD_STEP_CHUNKS):
        rs = slice(ci * SSM_CHUNK, (ci + 1) * SSM_CHUNK)

        def emit(g, y, rs=rs):
            o_ref[rs, g * GROUP_W:(g + 1) * GROUP_W] = y

        _ssd_chunk(u_ref, dt_ref, cst_ref, spread_ref, ht_ref, emit, rs, reverse=False)


def _ssd_bwd_kernel(u_ref, dt_ref, cst_ref, spread_ref, yf_ref, z0_ref, z1_ref, dsk_ref, nw_ref, o_ref, ht_ref, y_ref):
    @pl.when(pl.program_id(0) == 0)
    def _():
        ht_ref[...] = jnp.zeros(ht_ref.shape, F32)

    half = SSM_GROUPS // 2

    for ci in reversed(range(SSD_STEP_CHUNKS)):
        rs = slice(ci * SSM_CHUNK, (ci + 1) * SSM_CHUNK)

        def emit(g, y, rs=rs):
            sl = slice(g * GROUP_W, (g + 1) * GROUP_W)
            z_ref = z0_ref if g < half else z1_ref
            zsl = slice((g % half) * GROUP_W, (g % half + 1) * GROUP_W)
            tot = y + yf_ref[rs, sl] + dsk_ref[:, sl] * u_ref[rs, sl].astype(F32)
            y_ref[rs, sl] = tot * _silu(z_ref[rs, zsl].astype(F32))

        _ssd_chunk(u_ref, dt_ref, cst_ref, spread_ref, ht_ref, emit, rs, reverse=True)
    yv = y_ref[...]
    o_ref[...] = (yv * lax.rsqrt(jnp.mean(yv * yv, axis=-1, keepdims=True) + EPS) * nw_ref[...]).astype(o_ref.dtype)


def ssd_branch(u, dt_raw, cst, p, d_exp, ssm_norm_w, n_ctx_rows):
    r = u.shape[0]
    step_rows = SSD_STEP_CHUNKS * SSM_CHUNK
    n_chunks = r // step_rows
    n_ctx_chunks = n_ctx_rows // step_rows
    n_lat = n_chunks - n_ctx_chunks
    state = pltpu.VMEM((SSM_GROUPS, SSM_STATE, GROUP_W), F32)

    def fwd_idx(i):
        return jnp.where(i < n_ctx_chunks, n_lat + i, i - n_ctx_chunks)

    def bwd_idx(i):
        return n_chunks - 1 - i

    heads_per_group = SSM_HEADS // SSM_GROUPS
    hh = np.arange(LANES)[None, None, :, None]
    owner = (np.arange(2)[:, None, None, None] * SSM_HEADS + np.arange(SSM_GROUPS)[None, :, None, None] * heads_per_group
             + np.arange(GROUP_W)[None, None, None, :] // SSM_HEAD_DIM)
    spread = jnp.asarray(hh == owner, BF16)

    def spread_spec(direction):
        return pl.BlockSpec((None, SSM_GROUPS, LANES, GROUP_W), lambda i: (direction, 0, 0, 0))

    ublk = (step_rows, XBC_W)
    yf = pl.pallas_call(
        _ssd_fwd_kernel,
        grid=(n_chunks,),
        in_specs=[pl.BlockSpec(ublk, lambda i: (fwd_idx(i), 0)),
                  pl.BlockSpec((step_rows, LANES), lambda i: (fwd_idx(i), 0)),
                  pl.BlockSpec((8, LANES), lambda i: (0, 0)),
                  spread_spec(0)],
        out_specs=pl.BlockSpec((step_rows, BRANCH_W), lambda i: (fwd_idx(i), 0)),
        out_shape=jax.ShapeDtypeStruct((r, BRANCH_W), F32),
        scratch_shapes=[state],
        compiler_params=_params(("arbitrary",), 32),
        name="ssd_fwd",
    )(u, dt_raw, cst, spread)

    zw = BRANCH_W // 2
    z0 = OFF_Z // zw
    return pl.pallas_call(
        _ssd_bwd_kernel,
        grid=(n_chunks,),
        in_specs=[pl.BlockSpec(ublk, lambda i: (bwd_idx(i), 0)),
                  pl.BlockSpec((step_rows, LANES), lambda i: (bwd_idx(i), 0)),
                  pl.BlockSpec((8, LANES), lambda i: (0, 0)),
                  spread_spec(1),
                  pl.BlockSpec((step_rows, BRANCH_W), lambda i: (bwd_idx(i), 0)),
                  pl.BlockSpec((step_rows, zw), lambda i: (bwd_idx(i), z0)),
                  pl.BlockSpec((step_rows, zw), lambda i: (bwd_idx(i), z0 + 1)),
                  pl.BlockSpec((1, BRANCH_W), lambda i: (0, 0)),
                  pl.BlockSpec((1, BRANCH_W), lambda i: (0, 0))],
        out_specs=pl.BlockSpec((step_rows, BRANCH_W), lambda i: (bwd_idx(i), 0)),
        out_shape=jax.ShapeDtypeStruct((r, BRANCH_W), BF16),
        scratch_shapes=[state, pltpu.VMEM((step_rows, BRANCH_W), F32)],
        compiler_params=_params(("arbitrary",), 32),
        name="ssd_bwd",
    )(u, dt_raw, cst, spread, yf, p, p, d_exp, ssm_norm_w.reshape(1, BRANCH_W))


def _qkprep_kernel(qk_ref, cos_ref, sin_ref, qw_ref, kw_ref, swap_ref, q_ref, k_ref):
    cosv = cos_ref[...]
    sinv = sin_ref[...]
    scale = HEAD_DIM ** -0.5 * LOG2E
    swap = swap_ref[...]
    for h in range(ATT_HEADS + KV_HEADS):
        t = qk_ref[:, h * HEAD_DIM:(h + 1) * HEAD_DIM].astype(F32)
        w = qw_ref[...] if h < ATT_HEADS else kw_ref[...]
        n = t * lax.rsqrt(jnp.mean(t * t, axis=-1, keepdims=True) + EPS) * w
        partner = jnp.dot(n.astype(BF16), swap, preferred_element_type=F32)
        rot = n * cosv + partner * sinv
        if h < ATT_HEADS:
            q_ref[:, h * HEAD_DIM:(h + 1) * HEAD_DIM] = (rot * scale).astype(q_ref.dtype)
        else:
            hk = h - ATT_HEADS
            k_ref[:, hk * HEAD_DIM:(hk + 1) * HEAD_DIM] = rot.astype(k_ref.dtype)


def qk_prep(p, cos_t, sin_t, qw, kw):
    r = p.shape[0]
    tr = 256
    kvw = KV_HEADS * HEAD_DIM
    lane = np.arange(HEAD_DIM)
    swap = jnp.asarray(lane[:, None] == (lane ^ 1)[None, :], BF16)
    return pl.pallas_call(
        _qkprep_kernel,
        grid=(r // tr,),
        in_specs=[pl.BlockSpec((tr, QK_W), lambda i: (i, OFF_QA // QK_W)),
                  pl.BlockSpec((tr, HEAD_DIM), lambda i: (i, 0)),
                  pl.BlockSpec((tr, HEAD_DIM), lambda i: (i, 0)),
                  pl.BlockSpec((1, HEAD_DIM), lambda i: (0, 0)),
                  pl.BlockSpec((1, HEAD_DIM), lambda i: (0, 0)),
                  pl.BlockSpec((HEAD_DIM, HEAD_DIM), lambda i: (0, 0))],
        out_specs=[pl.BlockSpec((tr, BRANCH_W), lambda i: (i, 0)),
                   pl.BlockSpec((tr, kvw), lambda i: (i, 0))],
        out_shape=[jax.ShapeDtypeStruct((r, BRANCH_W), BF16),
                   jax.ShapeDtypeStruct((r, kvw), BF16)],
        compiler_params=_params(("arbitrary",), 32),
        name="qk_prep",
    )(p, cos_t, sin_t, qw, kw, swap)


def _flash_kernel(q_ref, k_ref, v_ref, g_ref, o_ref, m_ref, l_ref, acc_ref, *, nk):
    ki = pl.program_id(2)

    @pl.when(ki == 0)
    def _():
        m_ref[...] = jnp.full(m_ref.shape, -jnp.inf, F32)
        l_ref[...] = jnp.zeros(l_ref.shape, F32)
        acc_ref[...] = jnp.zeros(acc_ref.shape, F32)

    k = k_ref[...]
    v1 = jnp.concatenate([v_ref[...], jnp.ones(v_ref.shape, v_ref.dtype)], axis=1)
    reps = k.shape[0] // LANES
    for r in range(ATT_GROUP):
        hs = slice(r * HEAD_DIM, (r + 1) * HEAD_DIM)
        s = lax.dot_general(q_ref[:, hs], k, NT_DIMS, preferred_element_type=F32)
        m_prev = m_ref[r]
        m_next = jnp.maximum(m_prev, jnp.max(s, axis=1, keepdims=True))
        p = jnp.exp2(s - jnp.concatenate([m_next] * reps, axis=1))
        alpha = jnp.exp2(m_prev - m_next)
        pv = jnp.dot(p.astype(v1.dtype), v1, preferred_element_type=F32)
        l_ref[r] = alpha * l_ref[r] + pv[:, HEAD_DIM:]
        acc_ref[r] = alpha * acc_ref[r] + pv[:, :HEAD_DIM]
        m_ref[r] = m_next

    @pl.when(ki == nk - 1)
    def _():
        for r in range(ATT_GROUP):
            hs = slice(r * HEAD_DIM, (r + 1) * HEAD_DIM)
            o = acc_ref[r] / l_ref[r]
            o_ref[:, hs] = (_silu(g_ref[:, hs].astype(F32)) * o).astype(o_ref.dtype)


def flash_gqa(qn, kn, p, *, q_rows, q_row0, k_rows, k_row0, tq, tk):
    nq = q_rows // tq
    nk = k_rows // tk
    qb0 = q_row0 // tq
    kb0 = k_row0 // tk
    gw = ATT_GROUP * HEAD_DIM
    return pl.pallas_call(
        functools.partial(_flash_kernel, nk=nk),
        grid=(KV_HEADS, nq, nk),
        in_specs=[pl.BlockSpec((tq, gw), lambda g, i, j: (qb0 + i, g)),
                  pl.BlockSpec((tk, HEAD_DIM), lambda g, i, j: (kb0 + j, g)),
                  pl.BlockSpec((tk, HEAD_DIM), lambda g, i, j: (kb0 + j, OFF_VA // HEAD_DIM + g)),
                  pl.BlockSpec((tq, gw), lambda g, i, j: (qb0 + i, OFF_GA // gw + g))],
        out_specs=pl.BlockSpec((tq, gw), lambda g, i, j: (i, g)),
        out_shape=jax.ShapeDtypeStruct((q_rows, BRANCH_W), BF16),
        scratch_shapes=[pltpu.VMEM((ATT_GROUP, tq, LANES), F32),
                        pltpu.VMEM((ATT_GROUP, tq, LANES), F32),
                        pltpu.VMEM((ATT_GROUP, tq, HEAD_DIM), F32)],
        compiler_params=_params(("arbitrary", "arbitrary", "arbitrary"), 48),
        name="flash_gqa",
    )(qn, kn, p, p)


NA_QROWS = 4
NA_KROWS = 12
NA_PATTERNS = 3


def _na_rel_row(pattern, qr, kr):
    if pattern == 0:
        return kr - qr + (NA_WIN_R - 1) if kr < NA_WIN_R else None
    if pattern == 1:
        return kr - qr + (NA_WIN_R - 1 - NA_WIN_R // 2) if 0 <= kr - qr < NA_WIN_R else None
    first = NA_KROWS - NA_WIN_R
    return kr - qr + (NA_WIN_R - 1) - (NA_KROWS - NA_QROWS) if kr >= first else None


def _na_kernel(q_ref, k_ref, v_ref, g_ref, tp_ref, o_ref, bias_ref, qs_ref, v1_ref, *, n_lat, n_ctx, grid_rows,
               prep_rows):
    qn = NA_QROWS * GRID_W
    win = NA_KROWS * GRID_W
    n_blocks = grid_rows // NA_QROWS
    rows = n_lat + n_ctx
    for r0 in range(0, rows, prep_rows):
        rs = slice(r0, r0 + prep_rows)
        qs_ref[rs, :] = (q_ref[rs, :].astype(F32) * (HEAD_DIM ** -0.5 * LOG2E)).astype(BF16)
        v1_ref[rs, 0:HEAD_DIM] = v_ref[rs, :]
        v1_ref[rs, HEAD_DIM:2 * HEAD_DIM] = jnp.ones((prep_rows, HEAD_DIM), BF16)
    kc = k_ref[n_lat:rows, :]
    vc = v1_ref[n_lat:rows, :]

    lane_lo = lax.broadcasted_iota(jnp.int32, (GRID_W, LANES), 1) < GRID_W
    for pattern in range(NA_PATTERNS):
        for qr in range(NA_QROWS):
            for kp in range(NA_KROWS // 2):
                ja = _na_rel_row(pattern, qr, 2 * kp)
                jb = _na_rel_row(pattern, qr, 2 * kp + 1)
                if ja is None and jb is None:
                    tile = jnp.full((GRID_W, LANES), NEG_BIAS, F32)
                elif jb is None:
                    tile = jnp.where(lane_lo, tp_ref[ja + 1], NEG_BIAS)
                elif ja is None:
                    tile = jnp.where(lane_lo, NEG_BIAS, tp_ref[jb])
                else:
                    tile = tp_ref[jb]
                bias_ref[pattern, qr * GRID_W:(qr + 1) * GRID_W, kp * LANES:(kp + 1) * LANES] = tile

    def block_body(b, carry):
        u0 = jnp.clip(b * NA_QROWS - NA_WIN_R // 2, 0, grid_rows - NA_KROWS)
        pattern = jnp.where(b == 0, 0, jnp.where(b == n_blocks - 1, 2, 1))
        qs = pl.ds(pl.multiple_of(b * qn, qn), qn)
        ws = pl.ds(pl.multiple_of(u0 * GRID_W, GRID_W), win)
        q = qs_ref[qs, :]
        s_loc = lax.dot_general(q, k_ref[ws, :], NT_DIMS, preferred_element_type=F32) + bias_ref[pattern]
        s_ctx = lax.dot_general(q, kc, NT_DIMS, preferred_element_type=F32)
        m = jnp.maximum(jnp.max(s_loc, axis=1, keepdims=True), jnp.max(s_ctx, axis=1, keepdims=True))
        pv = (jnp.dot(jnp.exp2(s_loc - m).astype(BF16), v1_ref[ws, :], preferred_element_type=F32)
              + jnp.dot(jnp.exp2(s_ctx - m).astype(BF16), vc, preferred_element_type=F32))
        o = pv[:, :HEAD_DIM] / pv[:, HEAD_DIM:]
        o_ref[qs, :] = (_silu(g_ref[qs, :].astype(F32)) * o).astype(o_ref.dtype)
        return carry

    lax.fori_loop(0, n_blocks, block_body, 0, unroll=16)

    qs = slice(n_lat, rows)
    s = lax.dot_general(qs_ref[qs, :], kc, NT_DIMS, preferred_element_type=F32)
    pvc = jnp.dot(jnp.exp2(s - jnp.max(s, axis=1, keepdims=True)).astype(BF16), vc, preferred_element_type=F32)
    oc = pvc[:, :HEAD_DIM] / pvc[:, HEAD_DIM:]
    o_ref[qs, :] = (_silu(g_ref[qs, :].astype(F32)) * oc).astype(o_ref.dtype)


def neighbourhood(p, tp, n_lat):
    r = p.shape[0]
    n_ctx = r - n_lat
    col = lambda off: (lambda h: (0, off // HEAD_DIM + h))
    return pl.pallas_call(
        functools.partial(_na_kernel, n_lat=n_lat, n_ctx=n_ctx, grid_rows=n_lat // GRID_W,
                          prep_rows=_pick(r, (1056, 640, 256, 128))),
        grid=(NA_HEADS,),
        in_specs=[pl.BlockSpec((r, HEAD_DIM), col(OFF_QN)),
                  pl.BlockSpec((r, HEAD_DIM), col(OFF_KN)),
                  pl.BlockSpec((r, HEAD_DIM), col(OFF_VN)),
                  pl.BlockSpec((r, HEAD_DIM), col(OFF_GN)),
                  pl.BlockSpec((None, 2 * NA_WIN_R, GRID_W, 2 * GRID_W), lambda h: (h, 0, 0, 0))],
        out_specs=pl.BlockSpec((r, HEAD_DIM), lambda h: (0, h)),
        out_shape=jax.ShapeDtypeStruct((r, BRANCH_W), BF16),
        scratch_shapes=[pltpu.VMEM((NA_PATTERNS, NA_QROWS * GRID_W, NA_KROWS * GRID_W), F32),
                        pltpu.VMEM((r, HEAD_DIM), BF16),
                        pltpu.VMEM((r, 2 * HEAD_DIM), BF16)],
        compiler_params=_params(("arbitrary",), 48),
        name="neighbourhood",
    )(p, p, p, p, tp)


def na_bias_table(rpb):
    n_rel = 2 * NA_WIN_C - 1
    c = np.arange(GRID_W)[:, None]
    kc = np.arange(GRID_W)[None, :]
    c0 = np.clip(c - NA_WIN_C // 2, 0, GRID_W - NA_WIN_C)
    valid = (kc >= c0) & (kc < c0 + NA_WIN_C)
    rel = kc - c + (NA_WIN_C - 1)
    onehot = (rel[None] == np.arange(n_rel)[:, None, None]) & valid[None]
    h, nr, _ = rpb.shape
    t = jnp.dot(rpb.astype(F32).reshape(h * nr, n_rel), jnp.asarray(onehot.reshape(n_rel, -1), F32),
                precision=lax.Precision.HIGHEST).reshape(h, nr, GRID_W, GRID_W)
    t = jnp.where(jnp.asarray(valid), t * LOG2E, NEG_BIAS)
    masked = jnp.full((h, 1, GRID_W, GRID_W), NEG_BIAS, F32)
    tpad = jnp.concatenate([masked, t, masked], axis=1)
    return jnp.concatenate([tpad[:, :-1], tpad[:, 1:]], axis=-1)


def _gateup_kernel(a_ref, b_ref, c_ref, wa_ref, wb_ref, wc_ref, ga_ref, gb_ref, gc_ref, o_ref):
    acc = None
    for br_ref, w_ref, g_ref in ((a_ref, wa_ref, ga_ref), (b_ref, wb_ref, gb_ref), (c_ref, wc_ref, gc_ref)):
        up = jnp.dot(br_ref[...], w_ref[...], preferred_element_type=F32)
        term = _sigmoid(g_ref[...].astype(F32)) * up
        acc = term if acc is None else acc + term
    o_ref[...] = acc.astype(o_ref.dtype)


def gate_up(br_a, br_b, br_c, w_up, p, layer, *, rows, row0s, p_row0):
    tm = _pick(rows, (1024, 512, 256))
    tn = 512

    def br_spec(row0):
        return pl.BlockSpec((tm, BRANCH_W), lambda i, j: (row0 // tm + i, 0))

    def w_spec(b):
        return pl.BlockSpec((None, None, BRANCH_W, tn), lambda i, j: (layer, b, 0, j))

    def g_spec(b):
        return pl.BlockSpec((tm, tn), lambda i, j: (p_row0 // tm + i, (OFF_GM + b * D_MODEL) // tn + j))

    return pl.pallas_call(
        _gateup_kernel,
        grid=(rows // tm, D_MODEL // tn),
        in_specs=[br_spec(row0s[0]), br_spec(row0s[1]), br_spec(row0s[2]),
                  w_spec(0), w_spec(1), w_spec(2), g_spec(0), g_spec(1), g_spec(2)],
        out_specs=pl.BlockSpec((tm, tn), lambda i, j: (i, j)),
        out_shape=jax.ShapeDtypeStruct((rows, D_MODEL), BF16),
        compiler_params=_params(("arbitrary", "arbitrary"), 56),
        name="gate_up",
    )(br_a, br_b, br_c, w_up, w_up, w_up, p, p, p)


def _outproj_kernel(s_ref, w_ref, x_ref, mod_ref, o_ref, *, mod_row):
    y = jnp.dot(s_ref[...], w_ref[...], preferred_element_type=F32)
    o_ref[...] = x_ref[...] + mod_ref[mod_row:mod_row + 1, :] * y


def out_proj(s, w_out, x, mod, layer, mod_row):
    rows = s.shape[0]
    tm = _pick(rows, (1024, 512, 256))
    tn = 512
    return pl.pallas_call(
        functools.partial(_outproj_kernel, mod_row=mod_row),
        grid=(rows // tm, D_MODEL // tn),
        in_specs=[pl.BlockSpec((tm, D_MODEL), lambda i, j: (i, 0)),
                  pl.BlockSpec((None, D_MODEL, tn), lambda i, j: (layer, 0, j)),
                  pl.BlockSpec((tm, tn), lambda i, j: (i, j)),
                  pl.BlockSpec((8, tn), lambda i, j: (0, 2 * D_MODEL // tn + j))],
        out_specs=pl.BlockSpec((tm, tn), lambda i, j: (i, j)),
        out_shape=jax.ShapeDtypeStruct((rows, D_MODEL), F32),
        compiler_params=_params(("arbitrary", "arbitrary"), 48),
        name="out_proj",
    )(s, w_out, x, mod)


def _final_norm_kernel(x_ref, w_ref, o_ref):
    xv = x_ref[...]
    o_ref[...] = xv * lax.rsqrt(jnp.mean(xv * xv, axis=-1, keepdims=True) + EPS) * w_ref[...]


def final_norm(x, w):
    t, d = x.shape
    tr = 256
    return pl.pallas_call(
        _final_norm_kernel,
        grid=(t // tr,),
        in_specs=[pl.BlockSpec((tr, d), lambda i: (i, 0)), pl.BlockSpec((1, d), lambda i: (0, 0))],
        out_specs=pl.BlockSpec((tr, d), lambda i: (i, 0)),
        out_shape=jax.ShapeDtypeStruct((t, d), F32),
        compiler_params=_params(("arbitrary",), 40),
        name="final_norm",
    )(x, w.reshape(1, d))


def _rope_tables(n_lat, n_ctx):
    pos = jnp.arange(n_lat, dtype=jnp.int32)
    row = (pos // GRID_W).astype(F32)
    colp = (pos % GRID_W).astype(F32)
    n_freq = HEAD_DIM // 4
    inv_freq = ROPE_THETA ** (-jnp.arange(n_freq, dtype=F32) / n_freq)
    ang = jnp.concatenate([row[:, None] * inv_freq, colp[:, None] * inv_freq], axis=-1)
    cos, sin = jnp.cos(ang), jnp.sin(ang)
    cos_t = jnp.repeat(cos, 2, axis=-1)
    sin_t = jnp.stack([-sin, sin], axis=-1).reshape(n_lat, HEAD_DIM)
    cos_t = jnp.concatenate([cos_t, jnp.ones((n_ctx, HEAD_DIM), F32)], axis=0)
    sin_t = jnp.concatenate([sin_t, jnp.zeros((n_ctx, HEAD_DIM), F32)], axis=0)
    return cos_t, sin_t


def hybrid_layer(x, ctx, sb, lp, cos_t, sin_t, update_ctx):
    t = x.shape[0]
    lc = ctx.shape[0]
    layer = lp["layer"]
    mod = modulation(sb, lp["w_mod"], lp["b_mod"], layer)
    h = norm_modulate(x, ctx, lp["norm_w"], mod)
    p = in_proj(h, lp["w_in_t"], lp["b_main"], layer)
    dt_raw = dt_proj(h, lp["w_in_t"], lp["b_dt"], layer)

    u = ssd_conv(p, lp["conv_w"], lp["conv_b"], t // SSM_CHUNK)
    br_a = ssd_branch(u, dt_raw, lp["ssd_cst"], p, lp["d_exp"], lp["ssm_norm_w"], lc)

    qn, kn = qk_prep(p, cos_t, sin_t, lp["q_norm_w"], lp["k_norm_w"])
    tq = _pick(t, (1024, 512, 256))
    tk = _pick(t + lc, (1408, 768, 640, 512, 384, 256))
    br_b = flash_gqa(qn, kn, p, q_rows=t, q_row0=0, k_rows=t + lc, k_row0=0, tq=tq, tk=tk)

    br_c = neighbourhood(p, lp["na_bias"], t)

    s = gate_up(br_a, br_b, br_c, lp["w_up"], p, layer, rows=t, row0s=(0, 0, 0), p_row0=0)
    x_new = out_proj(s, lp["w_out"], x, mod, layer, 0)
    if not update_ctx:
        return x_new, ctx
    br_b_c = flash_gqa(qn, kn, p, q_rows=lc, q_row0=t, k_rows=lc, k_row0=t, tq=lc, tk=lc)
    s_c = gate_up(br_a, br_b_c, br_c, lp["w_up"], p, layer, rows=lc, row0s=(t, 0, t), p_row0=t)
    ctx_new = out_proj(s_c, lp["w_out"], ctx, mod, layer, 1)
    return x_new, ctx_new


def layer_params(layer, w_mod, b_mod, norm_w, w_in_t, b_in, conv_w, conv_b, a_log, dt_bias, d_skip, ssm_norm_w,
                 q_norm_w, k_norm_w, rpb, w_up, w_out):
    pad_heads = [(0, 0), (0, LANES - DT_W)]
    b = b_in[layer].astype(F32)
    cst = jnp.concatenate([
        jnp.pad(dt_bias[layer].astype(F32).reshape(1, DT_W), pad_heads),
        jnp.pad(-jnp.exp(a_log[layer].astype(F32)).reshape(1, DT_W), pad_heads),
        jnp.zeros((6, LANES), F32)], axis=0)
    return dict(
        layer=layer, w_mod=w_mod, b_mod=b_mod[layer], norm_w=norm_w[layer], w_in_t=w_in_t,
        b_main=jnp.concatenate([b[:_SRC_DT], b[_SRC_QA:]]),
        b_dt=jnp.pad(b[_SRC_DT:_SRC_QA], (0, LANES - DT_W)),
        conv_w=conv_w[layer], conv_b=conv_b[layer], ssd_cst=cst,
        d_exp=jnp.repeat(d_skip[layer].astype(F32), SSM_HEAD_DIM).reshape(1, BRANCH_W),
        ssm_norm_w=ssm_norm_w[layer],
        q_norm_w=q_norm_w[layer].reshape(1, HEAD_DIM), k_norm_w=k_norm_w[layer].reshape(1, HEAD_DIM),
        na_bias=na_bias_table(rpb[layer]), w_up=w_up, w_out=w_out)


def kernel(x, c, ctx, c_ctx, norm_w, w_mod, b_mod, w_in, b_in, conv_w, conv_b, a_log, dt_bias, d_skip, ssm_norm_w, q_norm_w, k_norm_w, rpb, w_up, w_out, final_norm_w):
    xs = x[0]
    cs = ctx[0]
    sb = jnp.broadcast_to(jnp.stack([c[0], c_ctx])[:, :, None], (2, D_MODEL, LANES))
    cos_t, sin_t = _rope_tables(xs.shape[0], cs.shape[0])
    w_in_t = jnp.swapaxes(w_in, 1, 2)
    w_up_b = w_up.astype(BF16)
    w_out_b = w_out.astype(BF16)
    for layer in range(DEPTH):
        lp = layer_params(layer, w_mod, b_mod, norm_w, w_in_t, b_in, conv_w, conv_b, a_log, dt_bias, d_skip,
                          ssm_norm_w, q_norm_w, k_norm_w, rpb, w_up_b, w_out_b)
        xs, cs = hybrid_layer(xs, cs, sb, lp, cos_t, sin_t, layer < DEPTH - 1)
    return final_norm(xs, final_norm_w)[None]
```
